```python
import math
import jax
import jax.numpy as jnp
from jax import lax
import numpy as np

D_MODEL = 2048
BATCH = 1
SEQ = 8192
DEPTH = 1
DEC_BATCH = 32
DEC_SEQ = 8
PAST_LEN = 8192
PAGE_SIZE = 128

MIX_WIDTH = D_MODEL
ATTN_WIDTH = MIX_WIDTH // 2
SSD_WIDTH = MIX_WIDTH - ATTN_WIDTH
HEAD_DIM = 64
ATTN_HEADS = ATTN_WIDTH // HEAD_DIM
MOBA_BLOCK = 256
MOBA_TOPK = 3
ATTN_Q_CHUNK = 32
SSD_HEAD_DIM = 64
SSD_HEADS = SSD_WIDTH // SSD_HEAD_DIM
SSD_GROUPS = 2
SSD_STATE = 128
SSD_CONV = 4
SSD_CHUNK = 128
CONV_CH = SSD_WIDTH + 2 * SSD_GROUPS * SSD_STATE
IN_COLS = 3 * ATTN_WIDTH + SSD_WIDTH + CONV_CH + SSD_HEADS
PEER_HEADS = 8
PEER_KEYS = 128
PEER_EXPERTS = PEER_KEYS * PEER_KEYS
PEER_TOPK = 16
PEER_KEY_DIM = 256
PEER_CHUNK = 64
RMS_EPS = 1e-6

kernel_name = "hymba_moba_ssd_peer_step"


def rmsnorm(x, w):
    xf = x.astype(jnp.float32)
    y = xf * lax.rsqrt(jnp.mean(xf * xf, axis=-1, keepdims=True) + RMS_EPS)
    return (y * w.astype(jnp.float32)).astype(x.dtype)


def adaln(c, w_ada, b_ada):
    mod = jax.nn.silu(c) @ w_ada + b_ada
    return jnp.split(mod[:, None, :], 6, axis=-1)


def causal_conv(u, buf, conv_w, conv_b):
    L = u.shape[1]
    full = jnp.concatenate([buf.astype(u.dtype), u], axis=1)
    y = conv_b
    for i in range(SSD_CONV):
        y = y + full[:, i:i + L] * conv_w[i]
    return jax.nn.silu(y), full[:, L:]


def segsum_exp(a):
    q = a.shape[-1]
    cs = jnp.cumsum(a, axis=-1)
    diff = cs[..., :, None] - cs[..., None, :]
    mask = jnp.tril(jnp.ones((q, q), dtype=bool))
    return jnp.exp(jnp.where(mask, diff, -jnp.inf))


def ssd_scan(x, dt, a, bm, cm, init_state):
    f32 = jnp.float32
    Bsz, L, H, P = x.shape
    q = min(SSD_CHUNK, L)
    nc = -(-L // q)
    pad = nc * q - L
    rep = SSD_HEADS // SSD_GROUPS
    bh = jnp.repeat(bm.astype(f32), rep, axis=2)
    ch = jnp.repeat(cm.astype(f32), rep, axis=2)
    xdt = x.astype(f32) * dt[..., None]
    da = dt * a

    def chunkify(t):
        t = jnp.pad(t, [(0, 0), (0, pad)] + [(0, 0)] * (t.ndim - 2))
        return t.reshape((Bsz, nc, q) + t.shape[2:])

    xdt, da, bh, ch = chunkify(xdt), chunkify(da), chunkify(bh), chunkify(ch)
    da = da.transpose(0, 3, 1, 2)
    cs = jnp.cumsum(da, axis=-1)
    scores = jnp.einsum('bclhn,bcshn->bhcls', ch, bh) * segsum_exp(da)
    y_diag = jnp.einsum('bhcls,bcshp->bclhp', scores, xdt)
    decay = jnp.exp(cs[..., -1:] - cs)
    states = jnp.einsum('bclhn,bhcl,bclhp->bchpn', bh, decay, xdt)
    chunk_decay = jnp.exp(cs[..., -1])

    def step(s, inp):
        st, dec = inp
        return dec[..., None, None] * s + st, s

    final, prev = lax.scan(step, init_state.astype(f32),
                           (states.transpose(1, 0, 2, 3, 4), chunk_decay.transpose(2, 0, 1)))
    prev = prev.transpose(1, 0, 2, 3, 4)
    y_off = jnp.einsum('bclhn,bchpn,bhcl->bclhp', ch, prev, jnp.exp(cs))
    y = (y_diag + y_off).reshape(Bsz, nc * q, H, P)[:, :L]
    return y.astype(x.dtype), final.astype(init_state.dtype)


def ssd_mixer(z, xbc, dt_raw, conv_buf, ssm_init, conv_w, conv_b, dt_bias, a_log, d_skip, ssd_norm_w):
    B, L, _ = xbc.shape
    xbc_c, new_buf = causal_conv(xbc, conv_buf, conv_w, conv_b)
    xs, bm, cm = jnp.split(xbc_c, [SSD_WIDTH, SSD_WIDTH + SSD_GROUPS * SSD_STATE], axis=-1)
    xs = xs.reshape(B, L, SSD_HEADS, SSD_HEAD_DIM)
    bm = bm.reshape(B, L, SSD_GROUPS, SSD_STATE)
    cm = cm.reshape(B, L, SSD_GROUPS, SSD_STATE)
    dt = jax.nn.softplus((dt_raw + dt_bias).astype(jnp.float32))
    a = -jnp.exp(a_log.astype(jnp.float32))
    y, new_state = ssd_scan(xs, dt, a, bm, cm, ssm_init)
    y = (y + xs * d_skip[:, None]).reshape(B, L, SSD_WIDTH)
    return rmsnorm(y * jax.nn.silu(z), ssd_norm_w), new_buf, new_state


def moba_attention(q, k, v, q_pos):
    f32 = jnp.float32
    B, Lq, H, dh = q.shape
    T = k.shape[1]
    nb = -(-T // MOBA_BLOCK)
    pad = nb * MOBA_BLOCK - T
    if pad:
        k = jnp.pad(k, ((0, 0), (0, pad), (0, 0), (0, 0)))
        v = jnp.pad(v, ((0, 0), (0, pad), (0, 0), (0, 0)))
    kb = k.reshape(B, nb, MOBA_BLOCK, H, dh)
    vb = v.reshape(B, nb, MOBA_BLOCK, H, dh)
    kmean = jnp.mean(kb.astype(f32), axis=2)
    n_sel = min(MOBA_TOPK, nb)
    scale = HEAD_DIM ** -0.5
    bi = jnp.arange(B)[:, None, None, None]
    hi = jnp.arange(H)[None, :, None, None]
    blk_ids = jnp.arange(nb)
    in_blk = jnp.arange(MOBA_BLOCK)

    def chunk_fn(args):
        qc, pc = args
        nq = qc.shape[1]
        cur = pc // MOBA_BLOCK
        s_blk = jnp.einsum('bqhd,bnhd->bhqn', qc.astype(f32), kmean)
        s_blk = jnp.where((blk_ids[None, :] < cur[:, None])[None, None], s_blk, -jnp.inf)
        _, sel = lax.top_k(s_blk, n_sel)
        own = jnp.broadcast_to(cur[None, None, :, None], (B, H, nq, 1)).astype(sel.dtype)
        idx = jnp.concatenate([sel, own], axis=-1)
        kg = kb[bi, idx, :, hi]
        vg = vb[bi, idx, :, hi]
        slot_ok = jnp.concatenate([jnp.arange(n_sel)[None, :] < cur[:, None],
                                   jnp.ones((nq, 1), dtype=bool)], axis=-1)
        kpos = idx[..., None] * MOBA_BLOCK + in_blk
        mask = (kpos <= pc[None, None, :, None, None]) & slot_ok[None, None, :, :, None]
        s = jnp.einsum('bqhd,bhqjkd->bhqjk', qc, kg, preferred_element_type=f32) * scale
        s = jnp.where(mask, s, -jnp.inf).reshape(B, H, nq, -1)
        p = jax.nn.softmax(s, axis=-1).reshape(kg.shape[:5]).astype(vg.dtype)
        return jnp.einsum('bhqjk,bhqjkd->bqhd', p, vg)

    qcl = min(ATTN_Q_CHUNK, Lq)
    n_chunks = -(-Lq // qcl)
    padq = n_chunks * qcl - Lq
    qp = jnp.pad(q, ((0, 0), (0, padq), (0, 0), (0, 0)))
    pp = jnp.pad(q_pos, (0, padq))
    qs = qp.reshape(B, n_chunks, qcl, H, dh).transpose(1, 0, 2, 3, 4)
    ps = pp.reshape(n_chunks, qcl)
    o = lax.map(chunk_fn, (qs, ps))
    return o.transpose(1, 0, 2, 3, 4).reshape(B, n_chunks * qcl, H, dh)[:, :Lq]


def peer_ffn(h, peer_w_query, peer_sub_keys, peer_u, peer_v):
    f32 = jnp.float32
    B, L, D = h.shape
    T = B * L
    tc = min(PEER_CHUNK, T)
    nt = -(-T // tc)
    tp = jnp.pad(h.reshape(T, D), ((0, nt * tc - T), (0, 0))).reshape(nt, tc, D)

    def chunk_fn(xc):
        qv = (xc @ peer_w_query).reshape(tc, PEER_HEADS, 2, PEER_KEY_DIM // 2)
        s = jnp.einsum('thsd,skd->thsk', qv, peer_sub_keys, preferred_element_type=f32)
        s1, i1 = lax.top_k(s[:, :, 0], PEER_TOPK)
        s2, i2 = lax.top_k(s[:, :, 1], PEER_TOPK)
        cand = (s1[..., :, None] + s2[..., None, :]).reshape(tc, PEER_HEADS, PEER_TOPK * PEER_TOPK)
        cidx = (i1[..., :, None] * PEER_KEYS + i2[..., None, :]).reshape(tc, PEER_HEADS, PEER_TOPK * PEER_TOPK)
        top, pos = lax.top_k(cand, PEER_TOPK)
        eidx = jnp.take_along_axis(cidx, pos, axis=-1)
        g = jax.nn.softmax(top, axis=-1)
        u = peer_u[eidx]
        act = jax.nn.gelu(jnp.einsum('td,thkd->thk', xc, u, preferred_element_type=f32))
        return jnp.einsum('thk,thkd->td', (g * act).astype(xc.dtype), peer_v[eidx])

    out = lax.map(chunk_fn, tp).reshape(nt * tc, D)[:T]
    return out.reshape(B, L, D)


def trunk_layer(x, c, k_past, v_past, conv_buf, ssm_init, w_ada, b_ada, norm_mix_pre, norm_mix_post,
                norm_ffn_pre, norm_ffn_post, w_in, conv_w, conv_b, dt_bias, a_log, d_skip, ssd_norm_w,
                w_out, peer_w_query, peer_sub_keys, peer_u, peer_v):
    B, L, _ = x.shape
    shift1, scale1, gate1, shift2, scale2, gate2 = adaln(c, w_ada, b_ada)
    h = rmsnorm(x, norm_mix_pre) * (1.0 + scale1) + shift1
    proj = h @ w_in
    cuts = [int(s) for s in np.cumsum([ATTN_WIDTH, ATTN_WIDTH, ATTN_WIDTH, SSD_WIDTH, CONV_CH])]
    q, k, v, z, xbc, dt_raw = jnp.split(proj, cuts, axis=-1)
    q = q.reshape(B, L, ATTN_HEADS, HEAD_DIM)
    k = k.reshape(B, L, ATTN_HEADS, HEAD_DIM)
    v = v.reshape(B, L, ATTN_HEADS, HEAD_DIM)
    past = k_past.shape[1]
    k_all = jnp.concatenate([k_past.astype(k.dtype), k], axis=1)
    v_all = jnp.concatenate([v_past.astype(v.dtype), v], axis=1)
    q_pos = past + jnp.arange(L, dtype=jnp.int32)
    attn = moba_attention(q, k_all, v_all, q_pos).reshape(B, L, ATTN_WIDTH)
    ssd, conv_new, ssm_new = ssd_mixer(z, xbc, dt_raw, conv_buf, ssm_init, conv_w, conv_b,
                                       dt_bias, a_log, d_skip, ssd_norm_w)
    mix = jnp.concatenate([attn, ssd.astype(attn.dtype)], axis=-1) @ w_out
    x = x + gate1 * rmsnorm(mix, norm_mix_post)
    h2 = rmsnorm(x, norm_ffn_pre) * (1.0 + scale2) + shift2
    x = x + gate2 * rmsnorm(peer_ffn(h2, peer_w_query, peer_sub_keys, peer_u, peer_v), norm_ffn_post)
    return x, k, v, conv_new, ssm_new


def setup_inputs(seed: int = 0) -> dict:
    key = jax.random.key(seed)
    ks = jax.random.split(key, 28)
    f32 = jnp.float32
    n_pages = PAST_LEN // PAGE_SIZE
    n_used = DEC_BATCH * n_pages
    n_pool = n_used + max(1, n_used // 4)

    def nrm(k, shape, s):
        return jax.random.normal(k, shape, f32) * s

    page_table = jax.random.permutation(ks[6], n_pool)[:n_used].reshape(DEC_BATCH, n_pages).astype(jnp.int32)
    dt0 = jnp.exp(jax.random.uniform(ks[15], (SSD_HEADS,), f32, math.log(1e-3), math.log(1e-1)))
    return {
        "x_prompt": nrm(ks[0], (BATCH, SEQ, D_MODEL), 1.0),
        "x_sample": nrm(ks[1], (DEC_BATCH, DEC_SEQ, D_MODEL), 1.0),
        "c_prompt": nrm(ks[2], (BATCH, D_MODEL), 1.0),
        "c_sample": nrm(ks[3], (DEC_BATCH, D_MODEL), 1.0),
        "cache_k": nrm(ks[4], (n_pool, PAGE_SIZE, ATTN_HEADS, HEAD_DIM), 1.0),
        "cache_v": nrm(ks[5], (n_pool, PAGE_SIZE, ATTN_HEADS, HEAD_DIM), 1.0),
        "page_table": page_table,
        "state_conv": nrm(ks[7], (DEC_BATCH, SSD_CONV - 1, CONV_CH), 1.0),
        "state_ssm": nrm(ks[8], (DEC_BATCH, SSD_HEADS, SSD_HEAD_DIM, SSD_STATE), 0.1),
        "w_ada": nrm(ks[9], (D_MODEL, 6 * D_MODEL), 0.5 * D_MODEL ** -0.5),
        "b_ada": nrm(ks[10], (6 * D_MODEL,), 0.02),
        "norm_mix_pre": 1.0 + nrm(ks[11], (D_MODEL,), 0.02),
        "norm_mix_post": 1.0 + nrm(ks[24], (D_MODEL,), 0.02),
        "norm_ffn_pre": 1.0 + nrm(ks[25], (D_MODEL,), 0.02),
        "norm_ffn_post": 1.0 + nrm(ks[26], (D_MODEL,), 0.02),
        "w_in": nrm(ks[12], (D_MODEL, IN_COLS), D_MODEL ** -0.5),
        "conv_w": nrm(ks[13], (SSD_CONV, CONV_CH), SSD_CONV ** -0.5),
        "conv_b": nrm(ks[14], (CONV_CH,), 0.02),
        "dt_bias": dt0 + jnp.log(-jnp.expm1(-dt0)),
        "a_log": jnp.log(jax.random.uniform(ks[16], (SSD_HEADS,), f32, 1.0, 16.0)),
        "d_skip": 1.0 + nrm(ks[17], (SSD_HEADS,), 0.02),
        "ssd_norm_w": 1.0 + nrm(ks[18], (SSD_WIDTH,), 0.02),
        "w_out": nrm(ks[19], (MIX_WIDTH, D_MODEL), MIX_WIDTH ** -0.5),
        "peer_w_query": nrm(ks[20], (D_MODEL, PEER_HEADS * PEER_KEY_DIM), D_MODEL ** -0.5),
        "peer_sub_keys": nrm(ks[21], (2, PEER_KEYS, PEER_KEY_DIM // 2), (PEER_KEY_DIM // 2) ** -0.5),
        "peer_u": nrm(ks[22], (PEER_EXPERTS, D_MODEL), D_MODEL ** -0.5),
        "peer_v": nrm(ks[23], (PEER_EXPERTS, D_MODEL), D_MODEL ** -0.5),
    }


def reference(x_prompt, x_sample, c_prompt, c_sample, cache_k, cache_v, page_table, state_conv, state_ssm,
              w_ada, b_ada, norm_mix_pre, norm_mix_post, norm_ffn_pre, norm_ffn_post, w_in, conv_w, conv_b,
              dt_bias, a_log, d_skip, ssd_norm_w, w_out, peer_w_query, peer_sub_keys, peer_u, peer_v):
    weights = (w_ada, b_ada, norm_mix_pre, norm_mix_post, norm_ffn_pre, norm_ffn_post, w_in, conv_w, conv_b,
               dt_bias, a_log, d_skip, ssd_norm_w, w_out, peer_w_query, peer_sub_keys, peer_u, peer_v)
    b_p = x_prompt.shape[0]
    b_s, n_pages = page_table.shape
    y_p = x_prompt
    y_s = x_sample
    for _ in range(DEPTH):
        k0 = jnp.zeros((b_p, 0, ATTN_HEADS, HEAD_DIM), x_prompt.dtype)
        conv0 = jnp.zeros((b_p, SSD_CONV - 1, CONV_CH), x_prompt.dtype)
        ssm0 = jnp.zeros((b_p, SSD_HEADS, SSD_HEAD_DIM, SSD_STATE), state_ssm.dtype)
        y_p, k_p, v_p, conv_p, ssm_p = trunk_layer(y_p, c_prompt, k0, k0, conv0, ssm0, *weights)
        k_past = cache_k[page_table].reshape(b_s, n_pages * cache_k.shape[1], ATTN_HEADS, HEAD_DIM)
        v_past = cache_v[page_table].reshape(b_s, n_pages * cache_v.shape[1], ATTN_HEADS, HEAD_DIM)
        y_s, k_s, v_s, conv_s, ssm_s = trunk_layer(y_s, c_sample, k_past, v_past, state_conv, state_ssm, *weights)
    return (y_p, y_s, k_p, v_p, conv_p, ssm_p, k_s, v_s, conv_s, ssm_s)
```

```python
import functools

import jax
import jax.numpy as jnp
from jax import lax
from jax.experimental import pallas as pl
from jax.experimental.pallas import tpu as pltpu

F32 = jnp.float32
BF16 = jnp.bfloat16

HEAD_DIM = 64
MOBA_BLOCK = 256
MOBA_TOPK = 3
SSD_HEAD_DIM = 64
SSD_GROUPS = 2
SSD_STATE = 128
SSD_CONV = 4
SSD_CHUNK = 128
PEER_HEADS = 8
PEER_KEYS = 128
PEER_TOPK = 16
RMS_EPS = 1e-6

LANES = 128
NEG = -1e30
VMEM_LIMIT = 56 * 1024 * 1024
W_PITCH = 136


def _cparams(*sem):
    return pltpu.CompilerParams(dimension_semantics=sem, vmem_limit_bytes=VMEM_LIMIT)


def _dot(a, b):
    return jnp.dot(a, b, preferred_element_type=F32)


def _dot_nt(a, b):
    return lax.dot_general(a, b, (((1,), (1,)), ((), ())), preferred_element_type=F32)


def _dot_tn(a, b):
    return lax.dot_general(a, b, (((0,), (0,)), ((), ())), preferred_element_type=F32)


def _split2(x):
    hi = x.astype(BF16)
    lo = (x - hi.astype(F32)).astype(BF16)
    return hi, lo


def _split3(x):
    hi = x.astype(BF16)
    r = x - hi.astype(F32)
    mid = r.astype(BF16)
    lo = (r - mid.astype(F32)).astype(BF16)
    return hi, mid, lo


def _dot3(a, b, dot=_dot):
    ah, al = _split2(a)
    bh, bl = _split2(b)
    return dot(ah, bh) + (dot(ah, bl) + dot(al, bh))


def _dot_exact_rhs(a, b_bf16, dot=_dot):
    h, m, l = _split3(a)
    return dot(h, b_bf16) + (dot(m, b_bf16) + dot(l, b_bf16))


def _silu(x):
    return x * jax.nn.sigmoid(x)


def _rms(x, w):
    return x * lax.rsqrt(jnp.mean(x * x, axis=-1, keepdims=True) + RMS_EPS) * w


def _topk_rows(cur, k, iota):
    n = cur.shape[0]
    vals, idxs = [], []
    for _ in range(k):
        m = jnp.max(cur, axis=0, keepdims=True)
        idx = jnp.min(jnp.where(cur == m, iota, n), axis=0, keepdims=True)
        vals.append(m)
        idxs.append(idx)
        cur = jnp.where(iota == idx, -jnp.inf, cur)
    return jnp.concatenate(vals, axis=0), jnp.concatenate(idxs, axis=0)


def _mod_kernel(c_ref, w_ref, b_ref, o_ref):
    s = _silu(c_ref[...]).astype(BF16)
    o_ref[...] = _dot(s, w_ref[...].astype(BF16)) + b_ref[...]


def _modulation(c_all, w_ada, b_ada):
    r, d = c_all.shape
    n = w_ada.shape[1]
    tn = 1536 if n % 1536 == 0 else n
    return pl.pallas_call(
        _mod_kernel,
        grid=(n // tn,),
        in_specs=[pl.BlockSpec((r, d), lambda j: (0, 0)),
                  pl.BlockSpec((d, tn), lambda j: (0, j)),
                  pl.BlockSpec((1, tn), lambda j: (0, j))],
        out_specs=pl.BlockSpec((r, tn), lambda j: (0, j)),
        out_shape=jax.ShapeDtypeStruct((r, n), F32),
        compiler_params=_cparams("arbitrary"),
        name="adaln_mod",
    )(c_all, w_ada, b_ada.reshape(1, n))


def _in_kernel(x_ref, nw_ref, sc_ref, sh_ref, w_ref, wdt_ref, o_ref, odt_ref, h_scr):
    @pl.when(pl.program_id(1) == 0)
    def _():
        h = _rms(x_ref[...], nw_ref[...]) * (1.0 + sc_ref[...]) + sh_ref[...]
        h_scr[...] = h.astype(BF16)
        odt_ref[...] = _dot(h_scr[...], wdt_ref[...])

    o_ref[...] = _dot(h_scr[...], w_ref[...])


def _in_proj(x, norm_w, scale, shift, w_main, w_dt, tm, tn):
    t, d = x.shape
    n = w_main.shape[1]
    mr = scale.shape[0]
    mod_spec = (pl.BlockSpec((1, d), lambda i, j: (0, 0)) if mr == 1
                else pl.BlockSpec((tm, d), lambda i, j: (i, 0)))
    return pl.pallas_call(
        _in_kernel,
        grid=(t // tm, n // tn),
        in_specs=[pl.BlockSpec((tm, d), lambda i, j: (i, 0)),
                  pl.BlockSpec((1, d), lambda i, j: (0, 0)),
                  mod_spec, mod_spec,
                  pl.BlockSpec((d, tn), lambda i, j: (0, j)),
                  pl.BlockSpec((d, LANES), lambda i, j: (0, 0))],
        out_specs=[pl.BlockSpec((tm, tn), lambda i, j: (i, j)),
                   pl.BlockSpec((tm, LANES), lambda i, j: (i, 0))],
        out_shape=[jax.ShapeDtypeStruct((t, n), F32),
                   jax.ShapeDtypeStruct((t, LANES), F32)],
        scratch_shapes=[pltpu.VMEM((tm, d), BF16)],
        compiler_params=_cparams("arbitrary", "arbitrary"),
        name="in_proj",
    )(x, norm_w.reshape(1, d), scale, shift, w_main, w_dt)


def _moba_prompt_kernel(q_ref, k_ref, v_ref, o_ref, kb_scr, vt_scr, kmean_scr, sel_scr, *, nb):
    i = pl.program_id(1)
    blk = MOBA_BLOCK

    @pl.when(i == 0)
    def _():
        for n in range(nb):
            kn = k_ref[pl.ds(n * blk, blk), :]
            kmean_scr[pl.ds(n, 1), :] = jnp.sum(kn, axis=0, keepdims=True) * (1.0 / blk)
            kb_scr[pl.ds(n * blk, blk), :] = kn.astype(BF16)
            vt_scr[:, pl.ds(n * blk, blk)] = v_ref[pl.ds(n * blk, blk), :].T.astype(BF16)

    q = q_ref[...]
    lane_head = lax.broadcasted_iota(jnp.int32, q.shape, 1) // HEAD_DIM
    blk_iota = lax.broadcasted_iota(jnp.int32, (nb, blk), 0)
    key_i = lax.broadcasted_iota(jnp.int32, (blk, blk), 0)
    qry_i = lax.broadcasted_iota(jnp.int32, (blk, blk), 1)
    row_head = lax.broadcasted_iota(jnp.int32, (2 * HEAD_DIM, blk), 0) // HEAD_DIM
    scale = HEAD_DIM ** -0.5
    out_t = jnp.zeros((2 * HEAD_DIM, blk), F32)

    for hh in range(2):
        qm = jnp.where(lane_head == hh, q, 0.0)
        s_blk = _dot3(kmean_scr[...], qm, _dot_nt)
        cur = jnp.where(blk_iota < i, s_blk, -jnp.inf)
        sel = jnp.zeros((nb, blk), F32)
        for s in range(min(MOBA_TOPK, nb)):
            m = jnp.max(cur, axis=0, keepdims=True)
            idx = jnp.min(jnp.where(cur == m, blk_iota, nb), axis=0, keepdims=True)
            hit = blk_iota == idx
            sel = jnp.where(hit, jnp.where(i > s, 1.0, sel), sel)
            cur = jnp.where(hit, -jnp.inf, cur)
        sel_scr[...] = sel
        qs = (qm * scale).astype(BF16)

        def step(kb, vt, mask, carry):
            m_run, l_run, acc = carry
            s_t = jnp.where(mask, _dot_nt(kb, qs), NEG)
            m_new = jnp.maximum(m_run, jnp.max(s_t, axis=0, keepdims=True))
            p = jnp.exp(s_t - m_new)
            alpha = jnp.exp(m_run - m_new)
            l_new = alpha * l_run + jnp.sum(p, axis=0, keepdims=True)
            acc_new = alpha * acc + _dot(vt, p.astype(BF16))
            return m_new, l_new, acc_new

        def body(n, carry):
            off = pl.multiple_of(n * blk, blk)
            mask = sel_scr[pl.ds(n, 1), :] > 0.5
            return step(kb_scr[pl.ds(off, blk), :], vt_scr[:, pl.ds(off, blk)], mask, carry)

        init = (jnp.full((1, blk), NEG, F32), jnp.zeros((1, blk), F32),
                jnp.zeros((2 * HEAD_DIM, blk), F32))
        carry = lax.fori_loop(0, i, body, init)
        off = pl.multiple_of(i * blk, blk)
        _, l_fin, acc = step(kb_scr[pl.ds(off, blk), :], vt_scr[:, pl.ds(off, blk)],
                             key_i <= qry_i, carry)
        out_t = jnp.where(row_head == hh, acc / l_fin, out_t)

    o_ref[...] = out_t.T


def _moba_prompt(proj, t, attn_w):
    nb = t // MOBA_BLOCK
    npair = attn_w // LANES
    kcol = attn_w // LANES
    return pl.pallas_call(
        functools.partial(_moba_prompt_kernel, nb=nb),
        grid=(npair, nb),
        in_specs=[pl.BlockSpec((MOBA_BLOCK, LANES), lambda p, i: (i, p)),
                  pl.BlockSpec((t, LANES), lambda p, i: (0, kcol + p)),
                  pl.BlockSpec((t, LANES), lambda p, i: (0, 2 * kcol + p))],
        out_specs=pl.BlockSpec((MOBA_BLOCK, LANES), lambda p, i: (i, p)),
        out_shape=jax.ShapeDtypeStruct((t, attn_w), F32),
        scratch_shapes=[pltpu.VMEM((t, LANES), BF16),
                        pltpu.VMEM((LANES, t), BF16),
                        pltpu.VMEM((nb, LANES), F32),
                        pltpu.VMEM((nb, MOBA_BLOCK), F32)],
        compiler_params=_cparams("arbitrary", "arbitrary"),
        name="moba_prompt",
    )(proj, proj, proj)


def _kmean_kernel(pt_ref, k0_ref, k1_ref, o_ref):
    n = pl.program_id(1)
    s = jnp.sum(k0_ref[0], axis=0, keepdims=True) + jnp.sum(k1_ref[0], axis=0, keepdims=True)
    o_ref[0, pl.ds(n, 1), :] = s * (1.0 / MOBA_BLOCK)


def _paged_kmean(cache_k, page_table, nb):
    b = page_table.shape[0]
    _, page, w = cache_k.shape
    assert 2 * page == MOBA_BLOCK
    return pl.pallas_call(
        _kmean_kernel,
        grid_spec=pltpu.PrefetchScalarGridSpec(
            num_scalar_prefetch=1,
            grid=(b, nb),
            in_specs=[pl.BlockSpec((1, page, w), lambda bi, n, pt: (pt[bi, 2 * n], 0, 0)),
                      pl.BlockSpec((1, page, w), lambda bi, n, pt: (pt[bi, 2 * n + 1], 0, 0))],
            out_specs=pl.BlockSpec((1, nb, w), lambda bi, n, pt: (bi, 0, 0))),
        out_shape=jax.ShapeDtypeStruct((b, nb, w), F32),
        compiler_params=_cparams("arbitrary", "arbitrary"),
        name="paged_kmean",
    )(page_table, cache_k, cache_k)


def _moba_sample_kernel(pt_ref, q_ref, kn_ref, vn_ref, kmean_ref, k0_ref, k1_ref, v0_ref, v1_ref,
                        o_ref, qbd_scr, sel_scr, m_scr, l_scr, acc_scr, *, nb, nh, lq):
    n = pl.program_id(1)
    r = nh * lq
    w = nh * HEAD_DIM
    scale = HEAD_DIM ** -0.5
    row_head = lax.broadcasted_iota(jnp.int32, (r, w), 0) // lq
    col_head = lax.broadcasted_iota(jnp.int32, (r, w), 1) // HEAD_DIM
    eye = (lax.broadcasted_iota(jnp.int32, (r, r), 0) == lax.broadcasted_iota(jnp.int32, (r, r), 1))

    def to_col(row):
        return jnp.sum(jnp.where(eye, row, 0.0), axis=1, keepdims=True)

    @pl.when(n == 0)
    def _():
        q = q_ref[0]
        qbd = jnp.where(row_head == col_head, jnp.concatenate([q] * nh, axis=0), 0.0)
        qbd_scr[...] = qbd
        s_blk = _dot3(kmean_ref[0], qbd, _dot_nt)
        blk_iota = lax.broadcasted_iota(jnp.int32, (nb, r), 0)
        cur = s_blk
        sel = jnp.zeros((nb, r), F32)
        for _ in range(min(MOBA_TOPK, nb)):
            m = jnp.max(cur, axis=0, keepdims=True)
            idx = jnp.min(jnp.where(cur == m, blk_iota, nb), axis=0, keepdims=True)
            hit = blk_iota == idx
            sel = jnp.where(hit, 1.0, sel)
            cur = jnp.where(hit, -jnp.inf, cur)
        sel_scr[...] = sel
        m_scr[...] = jnp.full(m_scr.shape, NEG, F32)
        l_scr[...] = jnp.zeros(l_scr.shape, F32)
        acc_scr[...] = jnp.zeros(acc_scr.shape, F32)

    qs = (qbd_scr[...] * scale).astype(BF16)

    def step(kb, vb, mask):
        s = jnp.where(mask, _dot_nt(qs, kb), NEG)
        m_old = m_scr[...]
        m_new = jnp.maximum(m_old, jnp.max(s, axis=1, keepdims=True))
        p = jnp.exp(s - m_new)
        alpha = jnp.exp(m_old - m_new)
        l_scr[...] = alpha * l_scr[...] + jnp.sum(p, axis=1, keepdims=True)
        acc_scr[...] = alpha * acc_scr[...] + _dot(p.astype(BF16), vb)
        m_scr[...] = m_new

    kb = jnp.concatenate([k0_ref[0], k1_ref[0]], axis=0).astype(BF16)
    vb = jnp.concatenate([v0_ref[0], v1_ref[0]], axis=0).astype(BF16)
    step(kb, vb, to_col(sel_scr[pl.ds(n, 1), :]) > 0.5)

    @pl.when(n == nb - 1)
    def _():
        qi = lax.broadcasted_iota(jnp.int32, (r, lq), 0) % lq
        ki = lax.broadcasted_iota(jnp.int32, (r, lq), 1)
        step(kn_ref[0].astype(BF16), vn_ref[0].astype(BF16), ki <= qi)
        full = jnp.where(row_head == col_head, acc_scr[...] / l_scr[...], 0.0)
        out = full[0:lq]
        for h in range(1, nh):
            out = out + full[h * lq:(h + 1) * lq]
        o_ref[0] = out


def _moba_sample(proj3, kmean, cache_k, cache_v, page_table, attn_w):
    b, lq, _ = proj3.shape
    nb = kmean.shape[1]
    _, page, w = cache_k.shape
    nh = attn_w // HEAD_DIM
    r = nh * lq
    pg0 = lambda bi, n, pt: (pt[bi, 2 * n], 0, 0)
    pg1 = lambda bi, n, pt: (pt[bi, 2 * n + 1], 0, 0)
    return pl.pallas_call(
        functools.partial(_moba_sample_kernel, nb=nb, nh=nh, lq=lq),
        grid_spec=pltpu.PrefetchScalarGridSpec(
            num_scalar_prefetch=1,
            grid=(b, nb),
            in_specs=[pl.BlockSpec((1, lq, attn_w), lambda bi, n, pt: (bi, 0, 0)),
                      pl.BlockSpec((1, lq, attn_w), lambda bi, n, pt: (bi, 0, 1)),
                      pl.BlockSpec((1, lq, attn_w), lambda bi, n, pt: (bi, 0, 2)),
                      pl.BlockSpec((1, nb, w), lambda bi, n, pt: (bi, 0, 0)),
                      pl.BlockSpec((1, page, w), pg0), pl.BlockSpec((1, page, w), pg1),
                      pl.BlockSpec((1, page, w), pg0), pl.BlockSpec((1, page, w), pg1)],
            out_specs=pl.BlockSpec((1, lq, attn_w), lambda bi, n, pt: (bi, 0, 0)),
            scratch_shapes=[pltpu.VMEM((r, w), F32),
                            pltpu.VMEM((nb, r), F32),
                            pltpu.VMEM((r, 1), F32),
                            pltpu.VMEM((r, 1), F32),
                            pltpu.VMEM((r, w), F32)]),
        out_shape=jax.ShapeDtypeStruct((b, lq, attn_w), F32),
        compiler_params=_cparams("arbitrary", "arbitrary"),
        name="moba_sample",
    )(page_table, proj3, proj3, proj3, kmean, cache_k, cache_k, cache_v, cache_v)


def _ssd_kernel(z_ref, xs_ref, bm_ref, cm_ref, dt_ref, conv0_ref, ssm0_ref,
                cw_ref, cb_ref, dtb_ref, alog_ref, dskip_ref, nw_ref, exp_h_ref, exp_r_ref,
                y_ref, ssm_ref, bx_scr, bb_scr, bc_scr, st_scr, *, lv, nheads):
    c = pl.program_id(1)
    q = SSD_CHUNK
    wx = xs_ref.shape[-1]
    ws = bm_ref.shape[-1]
    pad = 8
    npair = wx // LANES
    rep = nheads // SSD_GROUPS

    @pl.when(c == 0)
    def _():
        bx_scr[...] = jnp.zeros(bx_scr.shape, F32)
        bb_scr[...] = jnp.zeros(bb_scr.shape, F32)
        bc_scr[...] = jnp.zeros(bc_scr.shape, F32)
        c0 = conv0_ref[0]
        lo = pad - (SSD_CONV - 1)
        bx_scr[lo:pad, :] = c0[:, :wx]
        bb_scr[lo:pad, :] = c0[:, wx:wx + ws]
        bc_scr[lo:pad, :] = c0[:, wx + ws:]
        st_scr[...] = ssm0_ref[0]

    def conv(src_ref, buf, c_lo, c_hi):
        buf[pad:pad + lv, :] = src_ref[...]
        acc = cb_ref[:, c_lo:c_hi]
        for i in range(SSD_CONV):
            acc = acc + buf[pl.ds(pad - (SSD_CONV - 1) + i, q), :] * cw_ref[i:i + 1, c_lo:c_hi]
        buf[pad - (SSD_CONV - 1):pad, :] = buf[pad + lv - (SSD_CONV - 1):pad + lv, :]
        return _silu(acc)

    xs = conv(xs_ref, bx_scr, 0, wx)
    bm = conv(bm_ref, bb_scr, wx, wx + ws)
    cm = conv(cm_ref, bc_scr, wx + ws, wx + 2 * ws)

    row = lax.broadcasted_iota(jnp.int32, (q, LANES), 0)
    if lv == q:
        dt_raw = dt_ref[...]
    else:
        dt_raw = jnp.concatenate([dt_ref[...], jnp.zeros((q - lv, LANES), F32)], axis=0)
    dt = jnp.where(row < lv, jax.nn.softplus(dt_raw + dtb_ref[...]), 0.0)
    da = dt * (-jnp.exp(alog_ref[...]))

    li = lax.broadcasted_iota(jnp.int32, (q, q), 0)
    si = lax.broadcasted_iota(jnp.int32, (q, q), 1)
    tril = li >= si
    tri_l = jnp.where(tril, 1.0, 0.0).astype(BF16)
    tri_u = jnp.where(li <= si, 1.0, 0.0).astype(BF16)
    cs = _dot_exact_rhs(da, tri_l, lambda a, t: _dot(t, a))
    cs_t = _dot_exact_rhs(da, tri_u, _dot_tn)
    exp_h = exp_h_ref[...]
    dt_full = _dot_exact_rhs(dt, exp_h)
    cs_full = _dot_exact_rhs(cs, exp_h)
    cs_rep = _dot_exact_rhs(cs, exp_r_ref[...])
    cs_last = cs_full[q - 1:q, :]
    xdt = xs * dt_full
    xdec = (xdt * jnp.exp(cs_last - cs_full)).astype(BF16)
    exp_cs = jnp.exp(cs_full)
    xdt_b = xdt.astype(BF16)
    lane_half = lax.broadcasted_iota(jnp.int32, (q, LANES), 1) // SSD_HEAD_DIM
    rowh = lax.broadcasted_iota(jnp.int32, (LANES, 1), 0) // SSD_HEAD_DIM
    last_t = cs_t[:, q - 1:q]

    g_scores = []
    for g in range(SSD_GROUPS):
        cg = cm[:, g * SSD_STATE:(g + 1) * SSD_STATE].astype(BF16)
        bg = bm[:, g * SSD_STATE:(g + 1) * SSD_STATE].astype(BF16)
        g_scores.append((cg, bg, _dot_nt(cg, bg)))

    ys = []
    for p in range(npair):
        h0 = 2 * p
        cg, bg, sc = g_scores[h0 // rep]
        sl = slice(p * LANES, (p + 1) * LANES)
        xp = xdt_b[:, sl]
        yd = jnp.zeros((q, LANES), F32)
        for hh in range(2):
            h = h0 + hh
            diff = cs_rep[:, h * LANES:h * LANES + q] - cs_t[h:h + 1, :]
            lmat = jnp.exp(jnp.where(tril, diff, NEG))
            mh = (sc * lmat).astype(BF16)
            yd = yd + _dot(mh, jnp.where(lane_half == hh, xp, jnp.zeros_like(xp)))
        st = st_scr[sl, :]
        y_off = _dot_nt(cg, st.astype(BF16)) * exp_cs[:, sl]
        ys.append(yd + y_off)
        dec_col = jnp.where(rowh == 0, jnp.exp(last_t[h0:h0 + 1, :]), jnp.exp(last_t[h0 + 1:h0 + 2, :]))
        st_scr[sl, :] = dec_col * st + _dot_tn(xdec[:, sl], bg)

    y = jnp.concatenate(ys, axis=1) + xs * dskip_ref[...]
    y_ref[...] = _rms(y[:lv] * _silu(z_ref[...]), nw_ref[...])
    ssm_ref[0] = st_scr[...]


def _ssd(proj, proj_dt, conv0, ssm0, conv_w, conv_b, dt_bias, a_log, d_skip, norm_w, b, l, ssd_w):
    ws = SSD_GROUPS * SSD_STATE
    nheads = ssd_w // SSD_HEAD_DIM
    assert nheads <= LANES and nheads % (2 * SSD_GROUPS) == 0 and SSD_STATE == LANES
    lv = min(l, SSD_CHUNK)
    nc = l // lv
    assert nc * lv == l
    attn_cols = proj.shape[1] - 2 * ssd_w - 2 * ws
    assert attn_cols % ssd_w == 0 and (attn_cols + 2 * ssd_w) % ws == 0
    zc = attn_cols // ssd_w
    xc = zc + 1
    bc = (attn_cols + 2 * ssd_w) // ws
    hpad = lambda v: jnp.pad(v.reshape(1, -1), ((0, 0), (0, LANES - nheads)))
    hid = jnp.arange(LANES)[:, None]
    exp_h = (hid == (jnp.arange(ssd_w)[None, :] // SSD_HEAD_DIM)).astype(BF16)
    exp_r = (hid == (jnp.arange(nheads * LANES)[None, :] // LANES)).astype(BF16)
    cch = conv_w.shape[1]
    rowblk = lambda bi, c: bi * nc + c
    const = lambda bi, c: (0, 0)
    return pl.pallas_call(
        functools.partial(_ssd_kernel, lv=lv, nheads=nheads),
        grid=(b, nc),
        in_specs=[pl.BlockSpec((lv, ssd_w), lambda bi, c: (rowblk(bi, c), zc)),
                  pl.BlockSpec((lv, ssd_w), lambda bi, c: (rowblk(bi, c), xc)),
                  pl.BlockSpec((lv, ws), lambda bi, c: (rowblk(bi, c), bc)),
                  pl.BlockSpec((lv, ws), lambda bi, c: (rowblk(bi, c), bc + 1)),
                  pl.BlockSpec((lv, LANES), lambda bi, c: (rowblk(bi, c), 0)),
                  pl.BlockSpec((1, SSD_CONV - 1, cch), lambda bi, c: (bi, 0, 0)),
                  pl.BlockSpec((1, ssd_w, SSD_STATE), lambda bi, c: (bi, 0, 0)),
                  pl.BlockSpec((SSD_CONV, cch), const),
                  pl.BlockSpec((1, cch), const),
                  pl.BlockSpec((1, LANES), const),
                  pl.BlockSpec((1, LANES), const),
                  pl.BlockSpec((1, ssd_w), const),
                  pl.BlockSpec((1, ssd_w), const),
                  pl.BlockSpec((LANES, ssd_w), const),
                  pl.BlockSpec((LANES, nheads * LANES), const)],
        out_specs=[pl.BlockSpec((lv, ssd_w), lambda bi, c: (rowblk(bi, c), 0)),
                   pl.BlockSpec((1, ssd_w, SSD_STATE), lambda bi, c: (bi, 0, 0))],
        out_shape=[jax.ShapeDtypeStruct((b * l, ssd_w), F32),
                   jax.ShapeDtypeStruct((b, ssd_w, SSD_STATE), F32)],
        scratch_shapes=[pltpu.VMEM((SSD_CHUNK + 8, ssd_w), F32),
                        pltpu.VMEM((SSD_CHUNK + 8, ws), F32),
                        pltpu.VMEM((SSD_CHUNK + 8, ws), F32),
                        pltpu.VMEM((ssd_w, SSD_STATE), F32)],
        compiler_params=_cparams("arbitrary", "arbitrary"),
        name="ssd_mixer",
    )(proj, proj, proj, proj, proj_dt, conv0, ssm0, conv_w, conv_b.reshape(1, cch),
      hpad(dt_bias), hpad(a_log), jnp.repeat(d_skip, SSD_HEAD_DIM).reshape(1, ssd_w),
      norm_w.reshape(1, ssd_w), exp_h, exp_r)


def _out_kernel(attn_ref, ssd_ref, x_ref, wa_ref, ws_ref, npost_ref, gate_ref, npre_ref, sc_ref, sh_ref,
                x1_ref, h2_ref):
    mix = _dot(attn_ref[...].astype(BF16), wa_ref[...]) + _dot(ssd_ref[...].astype(BF16), ws_ref[...])
    x1 = x_ref[...] + gate_ref[...] * _rms(mix, npost_ref[...])
    x1_ref[...] = x1
    h2_ref[...] = (_rms(x1, npre_ref[...]) * (1.0 + sc_ref[...]) + sh_ref[...]).astype(BF16)


def _out_proj(attn, ssd, x, w_attn, w_ssd, n_post, gate, n_pre, scale, shift, tm):
    t, d = x.shape
    aw, sw = attn.shape[1], ssd.shape[1]
    mr = gate.shape[0]
    mod_spec = (pl.BlockSpec((1, d), lambda i: (0, 0)) if mr == 1 else pl.BlockSpec((tm, d), lambda i: (i, 0)))
    vec = pl.BlockSpec((1, d), lambda i: (0, 0))
    return pl.pallas_call(
        _out_kernel,
        grid=(t // tm,),
        in_specs=[pl.BlockSpec((tm, aw), lambda i: (i, 0)),
                  pl.BlockSpec((tm, sw), lambda i: (i, 0)),
                  pl.BlockSpec((tm, d), lambda i: (i, 0)),
                  pl.BlockSpec((aw, d), lambda i: (0, 0)),
                  pl.BlockSpec((sw, d), lambda i: (0, 0)),
                  vec, mod_spec, vec, mod_spec, mod_spec],
        out_specs=[pl.BlockSpec((tm, d), lambda i: (i, 0)),
                   pl.BlockSpec((tm, d), lambda i: (i, 0))],
        out_shape=[jax.ShapeDtypeStruct((t, d), F32), jax.ShapeDtypeStruct((t, d), BF16)],
        compiler_params=_cparams("arbitrary"),
        name="out_proj",
    )(attn, ssd, x, w_attn, w_ssd, n_post.reshape(1, d), gate, n_pre.reshape(1, d), scale, shift)


def _route_kernel(h_ref, wq_ref, sk_ref, a_ref, b_ref, g_ref):
    tt = h_ref.shape[0]
    kd = sk_ref.shape[2]
    q = _dot(h_ref[...], wq_ref[...])
    key_iota = lax.broadcasted_iota(jnp.int32, (PEER_KEYS, tt), 0)
    cand_iota = lax.broadcasted_iota(jnp.int32, (PEER_TOPK * PEER_TOPK, tt), 0)
    a_rows, b_rows, g_rows = [], [], []
    for h in range(PEER_HEADS):
        tops = []
        for s in range(2):
            qhs = q[:, (2 * h + s) * kd:(2 * h + s + 1) * kd]
            sc_t = _dot3(sk_ref[s], qhs, _dot_nt)
            tops.append(_topk_rows(sc_t, PEER_TOPK, key_iota))
        (s1, i1), (s2, i2) = tops
        cand = jnp.concatenate([s1[ka:ka + 1, :] + s2 for ka in range(PEER_TOPK)], axis=0)
        top, pos = _topk_rows(cand, PEER_TOPK, cand_iota)
        pa, pb = pos >> 4, pos & (PEER_TOPK - 1)
        ai = jnp.zeros_like(pos)
        bi = jnp.zeros_like(pos)
        for kk in range(PEER_TOPK):
            ai = jnp.where(pa == kk, i1[kk:kk + 1, :], ai)
            bi = jnp.where(pb == kk, i2[kk:kk + 1, :], bi)
        e = jnp.exp(top - jnp.max(top, axis=0, keepdims=True))
        g_rows.append(e / jnp.sum(e, axis=0, keepdims=True))
        a_rows.append(ai)
        b_rows.append(bi)
    a_ref[...] = jnp.concatenate(a_rows, axis=0).astype(F32).T.astype(jnp.int32)
    b_ref[...] = jnp.concatenate(b_rows, axis=0).astype(F32).T.astype(jnp.int32)
    g_ref[...] = jnp.concatenate(g_rows, axis=0).T


def _peer_route(h2, wq, sub_keys, tt):
    t, d = h2.shape
    nq = wq.shape[1]
    ns = PEER_HEADS * PEER_TOPK
    assert ns == LANES and sub_keys.shape[1] == PEER_KEYS == LANES
    tok = pl.BlockSpec((tt, ns), lambda i: (i, 0))
    return pl.pallas_call(
        _route_kernel,
        grid=(t // tt,),
        in_specs=[pl.BlockSpec((tt, d), lambda i: (i, 0)),
                  pl.BlockSpec((d, nq), lambda i: (0, 0)),
                  pl.BlockSpec(sub_keys.shape, lambda i: (0, 0, 0))],
        out_specs=[tok, tok, tok],
        out_shape=[jax.ShapeDtypeStruct((t, ns), jnp.int32), jax.ShapeDtypeStruct((t, ns), jnp.int32),
                   jax.ShapeDtypeStruct((t, ns), F32)],
        compiler_params=_cparams("arbitrary"),
        name="peer_route",
    )(h2, wq, sub_keys)


def _peer_kernel(h_ref, a_ref, b_ref, g_ref, u_ref, v_ref, o_ref, w_scr, acc_scr, *, na):
    j = pl.program_id(1)
    tt = h_ref.shape[0]
    nk = PEER_KEYS

    @pl.when(j == 0)
    def _():
        sub = lax.broadcasted_iota(jnp.int32, (nk, LANES), 0)

        def build(t, carry):
            arow = a_ref[pl.ds(t, 1), :]
            brow = b_ref[pl.ds(t, 1), :]
            grow = g_ref[pl.ds(t, 1), :]
            oa = jnp.where(sub == arow, 1.0, 0.0).astype(BF16)
            zb = jnp.where(sub == brow, grow, 0.0)
            zh, zl = _split2(zb)
            wt = _dot_nt(jnp.concatenate([oa, oa], axis=1), jnp.concatenate([zh, zl], axis=1))
            w_scr[pl.ds(pl.multiple_of(t * W_PITCH, 8), nk), :] = wt
            return carry

        lax.fori_loop(0, tt, build, 0)
        acc_scr[...] = jnp.zeros(acc_scr.shape, F32)

    x = h_ref[...]
    act = _dot_nt(x, u_ref[...])
    act = jax.nn.gelu(act)
    ws = [w_scr[pl.ds(j * na + a, tt, stride=W_PITCH), :] for a in range(na)]
    p = (jnp.concatenate(ws, axis=1) * act).astype(BF16)
    acc_scr[...] += _dot(p, v_ref[...])

    @pl.when(j == pl.num_programs(1) - 1)
    def _():
        o_ref[...] = acc_scr[...]


def _peer_experts(h2, a_idx, b_idx, g, u, v, tt, na):
    t, d = h2.shape
    ne = u.shape[0]
    eb = na * PEER_KEYS
    ns = a_idx.shape[1]
    tok = pl.BlockSpec((tt, ns), lambda i, j: (i, 0))
    return pl.pallas_call(
        functools.partial(_peer_kernel, na=na),
        grid=(t // tt, ne // eb),
        in_specs=[pl.BlockSpec((tt, d), lambda i, j: (i, 0)), tok, tok, tok,
                  pl.BlockSpec((eb, d), lambda i, j: (j, 0)),
                  pl.BlockSpec((eb, d), lambda i, j: (j, 0))],
        out_specs=pl.BlockSpec((tt, d), lambda i, j: (i, 0)),
        out_shape=jax.ShapeDtypeStruct((t, d), F32),
        scratch_shapes=[pltpu.VMEM((tt * W_PITCH, LANES), F32),
                        pltpu.VMEM((tt, d), F32)],
        compiler_params=_cparams("arbitrary", "arbitrary"),
        name="peer_experts",
    )(h2, a_idx, b_idx, g, u, v)


def _final_kernel(x_ref, f_ref, nw_ref, gate_ref, o_ref):
    o_ref[...] = x_ref[...] + gate_ref[...] * _rms(f_ref[...], nw_ref[...])


def _final(x1, ffn, n_post, gate, tm):
    t, d = x1.shape
    mr = gate.shape[0]
    mod_spec = (pl.BlockSpec((1, d), lambda i: (0, 0)) if mr == 1 else pl.BlockSpec((tm, d), lambda i: (i, 0)))
    return pl.pallas_call(
        _final_kernel,
        grid=(t // tm,),
        in_specs=[pl.BlockSpec((tm, d), lambda i: (i, 0)), pl.BlockSpec((tm, d), lambda i: (i, 0)),
                  pl.BlockSpec((1, d), lambda i: (0, 0)), mod_spec],
        out_specs=pl.BlockSpec((tm, d), lambda i: (i, 0)),
        out_shape=jax.ShapeDtypeStruct((t, d), F32),
        compiler_params=_cparams("arbitrary"),
        name="final_residual",
    )(x1, ffn, n_post.reshape(1, d), gate)


def _tile(n, pref):
    t = min(n, pref)
    while n % t:
        t //= 2
    return t


def kernel(x_prompt, x_sample, c_prompt, c_sample, cache_k, cache_v, page_table, state_conv, state_ssm,
           w_ada, b_ada, norm_mix_pre, norm_mix_post, norm_ffn_pre, norm_ffn_post, w_in, conv_w, conv_b,
           dt_bias, a_log, d_skip, ssd_norm_w, w_out, peer_w_query, peer_sub_keys, peer_u, peer_v):
    bp, lp, d = x_prompt.shape
    bs, ls, _ = x_sample.shape
    assert bp == 1 and lp % MOBA_BLOCK == 0 and ls <= SSD_CHUNK
    attn_w = d // 2
    ssd_w = d - attn_w
    nh_a = attn_w // HEAD_DIM
    nh_s = ssd_w // SSD_HEAD_DIM
    cch = conv_w.shape[1]
    n_main = 3 * attn_w + ssd_w + cch
    n_pool, page, _, _ = cache_k.shape
    n_pages = page_table.shape[1]
    past = n_pages * page
    assert past % MOBA_BLOCK == 0
    nb_past = past // MOBA_BLOCK
    tp, ts = bp * lp, bs * ls

    rows = bp + bs
    rpad = -rows % 8
    c_all = jnp.pad(jnp.concatenate([c_prompt, c_sample], axis=0), ((0, rpad), (0, 0)))
    mod = _modulation(c_all, w_ada, b_ada)
    mod_p = [mod[0:bp, k * d:(k + 1) * d] for k in range(6)]
    mod_s = [jnp.repeat(mod[bp:rows, k * d:(k + 1) * d], ls, axis=0) for k in range(6)]

    w_main = w_in[:, :n_main].astype(BF16)
    w_dt = jnp.pad(w_in[:, n_main:], ((0, 0), (0, LANES - nh_s))).astype(BF16)
    w_attn = w_out[:attn_w].astype(BF16)
    w_ssd = w_out[attn_w:].astype(BF16)
    wq = peer_w_query.astype(BF16)
    u_b = peer_u.astype(BF16)
    v_b = peer_v.astype(BF16)

    xp = x_prompt.reshape(tp, d)
    xs = x_sample.reshape(ts, d)
    tn = _tile(n_main, 512)
    proj_p, dt_p = _in_proj(xp, norm_mix_pre, mod_p[1], mod_p[0], w_main, w_dt, _tile(tp, 1024), tn)
    proj_s, dt_s = _in_proj(xs, norm_mix_pre, mod_s[1], mod_s[0], w_main, w_dt, _tile(ts, 256), tn)

    attn_p = _moba_prompt(proj_p, lp, attn_w)
    ck = cache_k.reshape(n_pool, page, attn_w)
    cv = cache_v.reshape(n_pool, page, attn_w)
    kmean = _paged_kmean(ck, page_table, nb_past)
    attn_s = _moba_sample(proj_s.reshape(bs, ls, n_main), kmean, ck, cv, page_table, attn_w).reshape(ts, attn_w)

    conv0_p = jnp.zeros((bp, SSD_CONV - 1, cch), F32)
    ssm0_p = jnp.zeros((bp, ssd_w, SSD_STATE), F32)
    ssd_args = (conv_w, conv_b, dt_bias, a_log, d_skip, ssd_norm_w)
    ssd_p, ssm_p = _ssd(proj_p, dt_p, conv0_p, ssm0_p, *ssd_args, bp, lp, ssd_w)
    ssd_s, ssm_s = _ssd(proj_s, dt_s, state_conv, state_ssm.reshape(bs, ssd_w, SSD_STATE), *ssd_args,
                        bs, ls, ssd_w)

    x1_p, h2_p = _out_proj(attn_p, ssd_p, xp, w_attn, w_ssd, norm_mix_post, mod_p[2], norm_ffn_pre,
                           mod_p[4], mod_p[3], _tile(tp, 512))
    x1_s, h2_s = _out_proj(attn_s, ssd_s, xs, w_attn, w_ssd, norm_mix_post, mod_s[2], norm_ffn_pre,
                           mod_s[4], mod_s[3], _tile(ts, 256))

    h2 = jnp.concatenate([h2_p, h2_s], axis=0)
    tt = _tile(tp + ts, 256)
    a_idx, b_idx, g = _peer_route(h2, wq, peer_sub_keys, tt)
    ffn = _peer_experts(h2, a_idx, b_idx, g, u_b, v_b, tt, 4)

    y_p = _final(x1_p, ffn[:tp], norm_ffn_post, mod_p[5], _tile(tp, 512))
    y_s = _final(x1_s, ffn[tp:], norm_ffn_post, mod_s[5], _tile(ts, 256))

    kcol, vcol, xcol = attn_w, 2 * attn_w, 3 * attn_w + ssd_w
    k_p = proj_p[:, kcol:kcol + attn_w].reshape(bp, lp, nh_a, HEAD_DIM)
    v_p = proj_p[:, vcol:vcol + attn_w].reshape(bp, lp, nh_a, HEAD_DIM)
    conv_p = proj_p[:, xcol:xcol + cch].reshape(bp, lp, cch)[:, lp - (SSD_CONV - 1):]
    k_s = proj_s[:, kcol:kcol + attn_w].reshape(bs, ls, nh_a, HEAD_DIM)
    v_s = proj_s[:, vcol:vcol + attn_w].reshape(bs, ls, nh_a, HEAD_DIM)
    xbc_s = proj_s[:, xcol:xcol + cch].reshape(bs, ls, cch)
    conv_s = jnp.concatenate([state_conv, xbc_s], axis=1)[:, ls:]
    return (y_p.reshape(bp, lp, d), y_s.reshape(bs, ls, d), k_p, v_p, conv_p,
            ssm_p.reshape(bp, nh_s, SSD_HEAD_DIM, SSD_STATE), k_s, v_s, conv_s,
            ssm_s.reshape(bs, nh_s, SSD_HEAD_DIM, SSD_STATE))
```

```python
import functools

import jax
import jax.numpy as jnp
from jax import lax
from jax.experimental import pallas as pl
from jax.experimental.pallas import tpu as pltpu

F32 = jnp.float32
BF16 = jnp.bfloat16

HEAD_DIM = 64
MOBA_BLOCK = 256
MOBA_TOPK = 3
SSD_HEAD_DIM = 64
SSD_GROUPS = 2
SSD_STATE = 128
SSD_CONV = 4
SSD_CHUNK = 128
PEER_HEADS = 8
PEER_KEYS = 128
PEER_TOPK = 16
RMS_EPS = 1e-6

LANES = 128
NEG = -1e30
VMEM_LIMIT = 56 * 1024 * 1024
MOBA_UNROLL = 4
W_PITCH = 72
PEER_BUILD_UNROLL = 8
PEER_TILES = (528, 2)


def _cparams(*sem):
    return pltpu.CompilerParams(dimension_semantics=sem, vmem_limit_bytes=VMEM_LIMIT)


def _dot(a, b):
    return jnp.dot(a, b, preferred_element_type=F32)


def _dot_nt(a, b):
    return lax.dot_general(a, b, (((1,), (1,)), ((), ())), preferred_element_type=F32)


def _dot_tn(a, b):
    return lax.dot_general(a, b, (((0,), (0,)), ((), ())), preferred_element_type=F32)


def _split2(x):
    hi = x.astype(BF16)
    lo = (x - hi.astype(F32)).astype(BF16)
    return hi, lo


def _split3(x):
    hi = x.astype(BF16)
    r = x - hi.astype(F32)
    mid = r.astype(BF16)
    lo = (r - mid.astype(F32)).astype(BF16)
    return hi, mid, lo


def _dot3(a, b, dot=_dot):
    ah, al = _split2(a)
    bh, bl = _split2(b)
    return dot(ah, bh) + (dot(ah, bl) + dot(al, bh))


def _dot_exact_rhs(a, b_bf16, dot=_dot):
    h, m, l = _split3(a)
    return dot(h, b_bf16) + (dot(m, b_bf16) + dot(l, b_bf16))


def _silu(x):
    return x * jax.nn.sigmoid(x)


def _rms(x, w):
    return x * lax.rsqrt(jnp.mean(x * x, axis=-1, keepdims=True) + RMS_EPS) * w


def _topk_rows(cur, k, iota):
    n = cur.shape[0]
    vals, idxs = [], []
    for _ in range(k):
        m = jnp.max(cur, axis=0, keepdims=True)
        idx = jnp.min(jnp.where(cur == m, iota, n), axis=0, keepdims=True)
        vals.append(m)
        idxs.append(idx)
        cur = jnp.where(iota == idx, -jnp.inf, cur)
    return jnp.concatenate(vals, axis=0), jnp.concatenate(idxs, axis=0)


def _mod_kernel(c_ref, w_ref, b_ref, o_ref):
    s = _silu(c_ref[...]).astype(BF16)
    o_ref[...] = _dot(s, w_ref[...].astype(BF16)) + b_ref[...]


def _modulation(c_all, w_ada, b_ada):
    r, d = c_all.shape
    n = w_ada.shape[1]
    tn = 1536 if n % 1536 == 0 else n
    return pl.pallas_call(
        _mod_kernel,
        grid=(n // tn,),
        in_specs=[pl.BlockSpec((r, d), lambda j: (0, 0)),
                  pl.BlockSpec((d, tn), lambda j: (0, j)),
                  pl.BlockSpec((1, tn), lambda j: (0, j))],
        out_specs=pl.BlockSpec((r, tn), lambda j: (0, j)),
        out_shape=jax.ShapeDtypeStruct((r, n), F32),
        compiler_params=_cparams("arbitrary"),
        name="adaln_mod",
    )(c_all, w_ada, b_ada.reshape(1, n))


def _in_kernel(x_ref, nw_ref, sc_ref, sh_ref, w_ref, wdt_ref, o_ref, odt_ref, h_scr):
    @pl.when(pl.program_id(1) == 0)
    def _():
        h = _rms(x_ref[...], nw_ref[...]) * (1.0 + sc_ref[...]) + sh_ref[...]
        h_scr[...] = h.astype(BF16)
        odt_ref[...] = _dot(h_scr[...], wdt_ref[...])

    o_ref[...] = _dot(h_scr[...], w_ref[...])


def _in_proj(x, norm_w, scale, shift, w_main, w_dt, tm, tn):
    t, d = x.shape
    n = w_main.shape[1]
    mr = scale.shape[0]
    mod_spec = (pl.BlockSpec((1, d), lambda i, j: (0, 0)) if mr == 1
                else pl.BlockSpec((tm, d), lambda i, j: (i, 0)))
    return pl.pallas_call(
        _in_kernel,
        grid=(t // tm, n // tn),
        in_specs=[pl.BlockSpec((tm, d), lambda i, j: (i, 0)),
                  pl.BlockSpec((1, d), lambda i, j: (0, 0)),
                  mod_spec, mod_spec,
                  pl.BlockSpec((d, tn), lambda i, j: (0, j)),
                  pl.BlockSpec((d, LANES), lambda i, j: (0, 0))],
        out_specs=[pl.BlockSpec((tm, tn), lambda i, j: (i, j)),
                   pl.BlockSpec((tm, LANES), lambda i, j: (i, 0))],
        out_shape=[jax.ShapeDtypeStruct((t, n), F32),
                   jax.ShapeDtypeStruct((t, LANES), F32)],
        scratch_shapes=[pltpu.VMEM((tm, d), BF16)],
        compiler_params=_cparams("arbitrary", "arbitrary"),
        name="in_proj",
    )(x, norm_w.reshape(1, d), scale, shift, w_main, w_dt)


def _moba_prompt_kernel(q_ref, k_ref, v_ref, o_ref, kb_scr, vt_scr, kmean_scr, sel_scr, *, nb):
    i = pl.program_id(1)
    blk = MOBA_BLOCK
    hd = HEAD_DIM

    @pl.when(i == 0)
    def _():
        row = lax.broadcasted_iota(jnp.int32, (2 * hd, blk), 0)
        ones_row = jnp.where(row == hd, 1.0, 0.0)
        for n in range(nb):
            kn = k_ref[pl.ds(n * blk, blk), :]
            kmean_scr[pl.ds(n, 1), :] = jnp.sum(kn, axis=0, keepdims=True) * (1.0 / blk)
            kb_scr[pl.ds(n * blk, blk), :] = kn.astype(BF16)
            vt = v_ref[pl.ds(n * blk, blk), :].T
            vt_scr[0, :, pl.ds(n * blk, blk)] = jnp.where(row < hd, vt, ones_row).astype(BF16)
            vt_sw = jnp.concatenate([vt[hd:], vt[:hd]], axis=0)
            vt_scr[1, :, pl.ds(n * blk, blk)] = jnp.where(row < hd, vt_sw, ones_row).astype(BF16)

    q = q_ref[...]
    lane_head = lax.broadcasted_iota(jnp.int32, q.shape, 1) // hd
    blk_iota = lax.broadcasted_iota(jnp.int32, (nb, blk), 0)
    key_i = lax.broadcasted_iota(jnp.int32, (blk, blk), 0)
    qry_i = lax.broadcasted_iota(jnp.int32, (blk, blk), 1)
    scale = hd ** -0.5

    qs = []
    for hh in range(2):
        qm = jnp.where(lane_head == hh, q, 0.0)
        s_blk = _dot3(kmean_scr[...], qm, _dot_nt)
        cur = jnp.where(blk_iota < i, s_blk, -jnp.inf)
        sel = jnp.zeros((nb, blk), F32)
        for s in range(min(MOBA_TOPK, nb)):
            m = jnp.max(cur, axis=0, keepdims=True)
            idx = jnp.min(jnp.where(cur == m, blk_iota, nb), axis=0, keepdims=True)
            hit = blk_iota == idx
            sel = jnp.where(hit, jnp.where(i > s, 1.0, sel), sel)
            cur = jnp.where(hit, -jnp.inf, cur)
        sel_scr[hh] = sel
        qs.append((qm * scale).astype(BF16))

    def tile(kb, vt, q_h, masks, m_run, acc):
        s_t = _dot_nt(kb, q_h)
        s_t = jnp.concatenate([jnp.where(mk, s_t[u * blk:(u + 1) * blk], NEG) for u, mk in enumerate(masks)], axis=0)
        m_new = jnp.maximum(m_run, jnp.max(s_t, axis=0, keepdims=True))
        p = jnp.exp(s_t - m_new).astype(BF16)
        return m_new, jnp.exp(m_run - m_new) * acc + _dot(vt, p)

    span = MOBA_UNROLL * blk

    def body(mi, carry):
        off = pl.multiple_of(mi * span, span)
        kb = kb_scr[pl.ds(off, span), :]
        out = []
        for hh in range(2):
            masks = [sel_scr[hh, pl.ds(mi * MOBA_UNROLL + u, 1), :] > 0.5 for u in range(MOBA_UNROLL)]
            out.extend(tile(kb, vt_scr[hh, :, pl.ds(off, span)], qs[hh], masks, carry[2 * hh], carry[2 * hh + 1]))
        return tuple(out)

    init = (jnp.full((1, blk), NEG, F32), jnp.zeros((2 * hd, blk), F32)) * 2
    carry = lax.fori_loop(0, (i + MOBA_UNROLL - 1) // MOBA_UNROLL, body, init)
    off = pl.multiple_of(i * blk, blk)
    kb = kb_scr[pl.ds(off, blk), :]
    outs = []
    for hh in range(2):
        _, acc = tile(kb, vt_scr[hh, :, pl.ds(off, blk)], qs[hh], [key_i <= qry_i], carry[2 * hh], carry[2 * hh + 1])
        outs.append(acc[:hd] / acc[hd:hd + 1])
    o_ref[...] = jnp.concatenate(outs, axis=0).T


def _moba_prompt(proj, t, attn_w):
    nb = t // MOBA_BLOCK
    assert nb % MOBA_UNROLL == 0
    npair = attn_w // LANES
    kcol = attn_w // LANES
    return pl.pallas_call(
        functools.partial(_moba_prompt_kernel, nb=nb),
        grid=(npair, nb),
        in_specs=[pl.BlockSpec((MOBA_BLOCK, LANES), lambda p, i: (i, p)),
                  pl.BlockSpec((t, LANES), lambda p, i: (0, kcol + p)),
                  pl.BlockSpec((t, LANES), lambda p, i: (0, 2 * kcol + p))],
        out_specs=pl.BlockSpec((MOBA_BLOCK, LANES), lambda p, i: (i, p)),
        out_shape=jax.ShapeDtypeStruct((t, attn_w), F32),
        scratch_shapes=[pltpu.VMEM((t, LANES), BF16),
                        pltpu.VMEM((2, LANES, t), BF16),
                        pltpu.VMEM((nb, LANES), F32),
                        pltpu.VMEM((2, nb, MOBA_BLOCK), F32)],
        compiler_params=_cparams("arbitrary", "arbitrary"),
        name="moba_prompt",
    )(proj, proj, proj)


def _moba_sample_kernel(pt_ref, q_ref, kn_ref, vn_ref, k0_ref, k1_ref, v0_ref, v1_ref, o_ref,
                        q2_scr, s_scr, mean_scr, max_scr, sel_scr, own_scr, m_scr, l_scr, acc_scr, *, nb, nh, lq):
    ph = pl.program_id(1)
    n = pl.program_id(2)
    r = nh * lq
    w = nh * HEAD_DIM
    blk = MOBA_BLOCK
    scale = HEAD_DIM ** -0.5
    row_head = lax.broadcasted_iota(jnp.int32, (r, w), 0) // lq
    col_head = lax.broadcasted_iota(jnp.int32, (r, w), 1) // HEAD_DIM
    lane = lax.broadcasted_iota(jnp.int32, (r, LANES), 1)
    own_ok = (lane < lq) & (lane <= lax.broadcasted_iota(jnp.int32, (r, LANES), 0) % lq)

    @pl.when((ph == 0) & (n == 0))
    def _():
        q = q_ref[0]
        qbd = jnp.where(row_head == col_head, jnp.concatenate([q] * nh, axis=0), 0.0)
        hi, lo = _split2(qbd)
        q2_scr[...] = jnp.concatenate([hi, lo], axis=0)
        mean_scr[...] = jnp.zeros(mean_scr.shape, F32)
        max_scr[...] = jnp.zeros(max_scr.shape, F32)

    @pl.when(ph == 0)
    def _():
        kt = jnp.concatenate([k0_ref[0], k1_ref[0]], axis=1).astype(BF16)
        s2 = _dot(q2_scr[...], kt)
        s = s2[:r] + s2[r:]
        s_scr[:, pl.ds(pl.multiple_of(n * blk, blk), blk)] = s
        mean_scr[...] = jnp.where(lane == n, jnp.sum(s, axis=1, keepdims=True) * (1.0 / blk), mean_scr[...])
        max_scr[...] = jnp.where(lane == n, jnp.max(s, axis=1, keepdims=True), max_scr[...])

    @pl.when((ph == 0) & (n == nb - 1))
    def _():
        cur = jnp.where(lane < nb, mean_scr[...], -jnp.inf)
        sel = jnp.zeros((r, LANES), F32)
        for _ in range(MOBA_TOPK):
            mx = jnp.max(cur, axis=1, keepdims=True)
            idx = jnp.min(jnp.where(cur == mx, lane, LANES), axis=1, keepdims=True)
            hit = lane == idx
            sel = jnp.where(hit, 1.0, sel)
            cur = jnp.where(hit, -jnp.inf, cur)
        sel_scr[...] = sel
        kn = jnp.concatenate([kn_ref[0], jnp.zeros((LANES - lq, w), F32)], axis=0).astype(BF16)
        o2 = _dot_nt(q2_scr[...], kn)
        own = jnp.where(own_ok, o2[:r] + o2[r:], NEG)
        own_scr[...] = own
        m_sel = jnp.max(jnp.where(sel > 0.5, max_scr[...], NEG), axis=1, keepdims=True)
        m_scr[...] = jnp.maximum(m_sel, jnp.max(own, axis=1, keepdims=True))
        l_scr[...] = jnp.zeros(l_scr.shape, F32)
        acc_scr[...] = jnp.zeros(acc_scr.shape, F32)

    @pl.when(ph == 1)
    def _():
        picked = jnp.sum(jnp.where(lane == n, sel_scr[...], 0.0), axis=1, keepdims=True) > 0.5
        s = s_scr[:, pl.ds(pl.multiple_of(n * blk, blk), blk)]
        p = jnp.where(picked, jnp.exp((s - m_scr[...]) * scale), 0.0)
        l_scr[...] += jnp.sum(p, axis=1, keepdims=True)
        vt = jnp.concatenate([v0_ref[0], v1_ref[0]], axis=1).astype(BF16)
        acc_scr[...] += _dot_nt(p.astype(BF16), vt)

    @pl.when((ph == 1) & (n == nb - 1))
    def _():
        p = jnp.where(own_ok, jnp.exp((own_scr[...] - m_scr[...]) * scale), 0.0)
        vn = jnp.concatenate([vn_ref[0], jnp.zeros((LANES - lq, w), F32)], axis=0).astype(BF16)
        l_fin = l_scr[...] + jnp.sum(p, axis=1, keepdims=True)
        acc = acc_scr[...] + _dot(p.astype(BF16), vn)
        full = jnp.where(row_head == col_head, acc / l_fin, 0.0)
        out = full[0:lq]
        for h in range(1, nh):
            out = out + full[h * lq:(h + 1) * lq]
        o_ref[0] = out


def _moba_sample(proj3, cache_kt, cache_vt, page_table, attn_w):
    b, lq, _ = proj3.shape
    _, w, page = cache_kt.shape
    n_pages = page_table.shape[1]
    nh = attn_w // HEAD_DIM
    r = nh * lq
    assert 2 * page == MOBA_BLOCK and page == LANES and w == attn_w and lq <= LANES and r % 8 == 0
    nb = n_pages // 2
    assert nb * 2 == n_pages and MOBA_TOPK <= nb <= LANES
    kmap = lambda o: (lambda bi, ph, n, pt: (pt[bi, 2 * (n * (1 - ph) + (nb - 1) * ph) + o], 0, 0))
    vmap = lambda o: (lambda bi, ph, n, pt: (pt[bi, 2 * (n * ph) + o], 0, 0))
    newspec = lambda c: pl.BlockSpec((1, lq, attn_w), lambda bi, ph, n, pt: (bi, 0, c))
    return pl.pallas_call(
        functools.partial(_moba_sample_kernel, nb=nb, nh=nh, lq=lq),
        grid_spec=pltpu.PrefetchScalarGridSpec(
            num_scalar_prefetch=1,
            grid=(b, 2, nb),
            in_specs=[newspec(0), newspec(1), newspec(2),
                      pl.BlockSpec((1, w, page), kmap(0)), pl.BlockSpec((1, w, page), kmap(1)),
                      pl.BlockSpec((1, w, page), vmap(0)), pl.BlockSpec((1, w, page), vmap(1))],
            out_specs=pl.BlockSpec((1, lq, attn_w), lambda bi, ph, n, pt: (bi, 0, 0)),
            scratch_shapes=[pltpu.VMEM((2 * r, w), BF16),
                            pltpu.VMEM((r, nb * MOBA_BLOCK), F32),
                            pltpu.VMEM((r, LANES), F32),
                            pltpu.VMEM((r, LANES), F32),
                            pltpu.VMEM((r, LANES), F32),
                            pltpu.VMEM((r, LANES), F32),
                            pltpu.VMEM((r, 1), F32),
                            pltpu.VMEM((r, 1), F32),
                            pltpu.VMEM((r, w), F32)]),
        out_shape=jax.ShapeDtypeStruct((b, lq, attn_w), F32),
        compiler_params=_cparams("arbitrary", "arbitrary", "arbitrary"),
        name="moba_sample",
    )(page_table, proj3, proj3, proj3, cache_kt, cache_kt, cache_vt, cache_vt)


def _ssd_kernel(z_ref, xs_ref, bm_ref, cm_ref, dt_ref, conv0_ref, ssm0_ref,
                cw_ref, cb_ref, dtb_ref, alog_ref, dskip_ref, nw_ref, exp_h_ref, exp_r_ref,
                y_ref, ssm_ref, bx_scr, bb_scr, bc_scr, st_scr, *, lv, nheads):
    c = pl.program_id(1)
    q = SSD_CHUNK
    wx = xs_ref.shape[-1]
    ws = bm_ref.shape[-1]
    pad = 8
    npair = wx // LANES
    rep = nheads // SSD_GROUPS

    @pl.when(c == 0)
    def _():
        bx_scr[...] = jnp.zeros(bx_scr.shape, F32)
        bb_scr[...] = jnp.zeros(bb_scr.shape, F32)
        bc_scr[...] = jnp.zeros(bc_scr.shape, F32)
        c0 = conv0_ref[0]
        lo = pad - (SSD_CONV - 1)
        bx_scr[lo:pad, :] = c0[:, :wx]
        bb_scr[lo:pad, :] = c0[:, wx:wx + ws]
        bc_scr[lo:pad, :] = c0[:, wx + ws:]
        st_scr[...] = ssm0_ref[0]

    def conv(src_ref, buf, c_lo, c_hi):
        buf[pad:pad + lv, :] = src_ref[...]
        acc = cb_ref[:, c_lo:c_hi]
        for i in range(SSD_CONV):
            acc = acc + buf[pl.ds(pad - (SSD_CONV - 1) + i, q), :] * cw_ref[i:i + 1, c_lo:c_hi]
        buf[pad - (SSD_CONV - 1):pad, :] = buf[pad + lv - (SSD_CONV - 1):pad + lv, :]
        return _silu(acc)

    xs = conv(xs_ref, bx_scr, 0, wx)
    bm = conv(bm_ref, bb_scr, wx, wx + ws)
    cm = conv(cm_ref, bc_scr, wx + ws, wx + 2 * ws)

    row = lax.broadcasted_iota(jnp.int32, (q, LANES), 0)
    if lv == q:
        dt_raw = dt_ref[...]
    else:
        dt_raw = jnp.concatenate([dt_ref[...], jnp.zeros((q - lv, LANES), F32)], axis=0)
    dt = jnp.where(row < lv, jax.nn.softplus(dt_raw + dtb_ref[...]), 0.0)
    da = dt * (-jnp.exp(alog_ref[...]))

    li = lax.broadcasted_iota(jnp.int32, (q, q), 0)
    si = lax.broadcasted_iota(jnp.int32, (q, q), 1)
    tril = li >= si
    tri_l = jnp.where(tril, 1.0, 0.0).astype(BF16)
    tri_u = jnp.where(li <= si, 1.0, 0.0).astype(BF16)
    cs = _dot_exact_rhs(da, tri_l, lambda a, t: _dot(t, a))
    cs_t = _dot_exact_rhs(da, tri_u, _dot_tn)
    exp_h = exp_h_ref[...]
    dt_full = _dot_exact_rhs(dt, exp_h)
    cs_full = _dot_exact_rhs(cs, exp_h)
    cs_rep = _dot_exact_rhs(cs, exp_r_ref[...])
    cs_last = cs_full[q - 1:q, :]
    xdt = xs * dt_full
    xdec = (xdt * jnp.exp(cs_last - cs_full)).astype(BF16)
    exp_cs = jnp.exp(cs_full)
    xdt_b = xdt.astype(BF16)
    lane_half = lax.broadcasted_iota(jnp.int32, (q, LANES), 1) // SSD_HEAD_DIM
    rowh = lax.broadcasted_iota(jnp.int32, (LANES, 1), 0) // SSD_HEAD_DIM
    last_t = cs_t[:, q - 1:q]

    g_scores = []
    for g in range(SSD_GROUPS):
        cg = cm[:, g * SSD_STATE:(g + 1) * SSD_STATE].astype(BF16)
        bg = bm[:, g * SSD_STATE:(g + 1) * SSD_STATE].astype(BF16)
        g_scores.append((cg, bg, _dot_nt(cg, bg)))

    ys = []
    for p in range(npair):
        h0 = 2 * p
        cg, bg, sc = g_scores[h0 // rep]
        sl = slice(p * LANES, (p + 1) * LANES)
        xp = xdt_b[:, sl]
        yd = jnp.zeros((q, LANES), F32)
        for hh in range(2):
            h = h0 + hh
            diff = cs_rep[:, h * LANES:h * LANES + q] - cs_t[h:h + 1, :]
            lmat = jnp.exp(jnp.where(tril, diff, NEG))
            mh = (sc * lmat).astype(BF16)
            yd = yd + _dot(mh, jnp.where(lane_half == hh, xp, jnp.zeros_like(xp)))
        st = st_scr[sl, :]
        y_off = _dot_nt(cg, st.astype(BF16)) * exp_cs[:, sl]
        ys.append(yd + y_off)
        dec_col = jnp.where(rowh == 0, jnp.exp(last_t[h0:h0 + 1, :]), jnp.exp(last_t[h0 + 1:h0 + 2, :]))
        st_scr[sl, :] = dec_col * st + _dot_tn(xdec[:, sl], bg)

    y = jnp.concatenate(ys, axis=1) + xs * dskip_ref[...]
    y_ref[...] = _rms(y[:lv] * _silu(z_ref[...]), nw_ref[...])
    ssm_ref[0] = st_scr[...]


def _ssd(proj, proj_dt, conv0, ssm0, conv_w, conv_b, dt_bias, a_log, d_skip, norm_w, b, l, ssd_w):
    ws = SSD_GROUPS * SSD_STATE
    nheads = ssd_w // SSD_HEAD_DIM
    assert nheads <= LANES and nheads % (2 * SSD_GROUPS) == 0 and SSD_STATE == LANES
    lv = min(l, SSD_CHUNK)
    nc = l // lv
    assert nc * lv == l
    attn_cols = proj.shape[1] - 2 * ssd_w - 2 * ws
    assert attn_cols % ssd_w == 0 and (attn_cols + 2 * ssd_w) % ws == 0
    zc = attn_cols // ssd_w
    xc = zc + 1
    bc = (attn_cols + 2 * ssd_w) // ws
    hpad = lambda v: jnp.pad(v.reshape(1, -1), ((0, 0), (0, LANES - nheads)))
    hid = jnp.arange(LANES)[:, None]
    exp_h = (hid == (jnp.arange(ssd_w)[None, :] // SSD_HEAD_DIM)).astype(BF16)
    exp_r = (hid == (jnp.arange(nheads * LANES)[None, :] // LANES)).astype(BF16)
    cch = conv_w.shape[1]
    rowblk = lambda bi, c: bi * nc + c
    const = lambda bi, c: (0, 0)
    return pl.pallas_call(
        functools.partial(_ssd_kernel, lv=lv, nheads=nheads),
        grid=(b, nc),
        in_specs=[pl.BlockSpec((lv, ssd_w), lambda bi, c: (rowblk(bi, c), zc)),
                  pl.BlockSpec((lv, ssd_w), lambda bi, c: (rowblk(bi, c), xc)),
                  pl.BlockSpec((lv, ws), lambda bi, c: (rowblk(bi, c), bc)),
                  pl.BlockSpec((lv, ws), lambda bi, c: (rowblk(bi, c), bc + 1)),
                  pl.BlockSpec((lv, LANES), lambda bi, c: (rowblk(bi, c), 0)),
                  pl.BlockSpec((1, SSD_CONV - 1, cch), lambda bi, c: (bi, 0, 0)),
                  pl.BlockSpec((1, ssd_w, SSD_STATE), lambda bi, c: (bi, 0, 0)),
                  pl.BlockSpec((SSD_CONV, cch), const),
                  pl.BlockSpec((1, cch), const),
                  pl.BlockSpec((1, LANES), const),
                  pl.BlockSpec((1, LANES), const),
                  pl.BlockSpec((1, ssd_w), const),
                  pl.BlockSpec((1, ssd_w), const),
                  pl.BlockSpec((LANES, ssd_w), const),
                  pl.BlockSpec((LANES, nheads * LANES), const)],
        out_specs=[pl.BlockSpec((lv, ssd_w), lambda bi, c: (rowblk(bi, c), 0)),
                   pl.BlockSpec((1, ssd_w, SSD_STATE), lambda bi, c: (bi, 0, 0))],
        out_shape=[jax.ShapeDtypeStruct((b * l, ssd_w), F32),
                   jax.ShapeDtypeStruct((b, ssd_w, SSD_STATE), F32)],
        scratch_shapes=[pltpu.VMEM((SSD_CHUNK + 8, ssd_w), F32),
                        pltpu.VMEM((SSD_CHUNK + 8, ws), F32),
                        pltpu.VMEM((SSD_CHUNK + 8, ws), F32),
                        pltpu.VMEM((ssd_w, SSD_STATE), F32)],
        compiler_params=_cparams("arbitrary", "arbitrary"),
        name="ssd_mixer",
    )(proj, proj, proj, proj, proj_dt, conv0, ssm0, conv_w, conv_b.reshape(1, cch),
      hpad(dt_bias), hpad(a_log), jnp.repeat(d_skip, SSD_HEAD_DIM).reshape(1, ssd_w),
      norm_w.reshape(1, ssd_w), exp_h, exp_r)


def _out_kernel(attn_ref, ssd_ref, x_ref, wa_ref, ws_ref, npost_ref, gate_ref, npre_ref, sc_ref, sh_ref,
                x1_ref, h2_ref):
    mix = _dot(attn_ref[...].astype(BF16), wa_ref[...]) + _dot(ssd_ref[...].astype(BF16), ws_ref[...])
    x1 = x_ref[...] + gate_ref[...] * _rms(mix, npost_ref[...])
    x1_ref[...] = x1
    h2_ref[...] = (_rms(x1, npre_ref[...]) * (1.0 + sc_ref[...]) + sh_ref[...]).astype(BF16)


def _out_proj(attn, ssd, x, w_attn, w_ssd, n_post, gate, n_pre, scale, shift, tm):
    t, d = x.shape
    aw, sw = attn.shape[1], ssd.shape[1]
    mr = gate.shape[0]
    mod_spec = (pl.BlockSpec((1, d), lambda i: (0, 0)) if mr == 1 else pl.BlockSpec((tm, d), lambda i: (i, 0)))
    vec = pl.BlockSpec((1, d), lambda i: (0, 0))
    return pl.pallas_call(
        _out_kernel,
        grid=(t // tm,),
        in_specs=[pl.BlockSpec((tm, aw), lambda i: (i, 0)),
                  pl.BlockSpec((tm, sw), lambda i: (i, 0)),
                  pl.BlockSpec((tm, d), lambda i: (i, 0)),
                  pl.BlockSpec((aw, d), lambda i: (0, 0)),
                  pl.BlockSpec((sw, d), lambda i: (0, 0)),
                  vec, mod_spec, vec, mod_spec, mod_spec],
        out_specs=[pl.BlockSpec((tm, d), lambda i: (i, 0)),
                   pl.BlockSpec((tm, d), lambda i: (i, 0))],
        out_shape=[jax.ShapeDtypeStruct((t, d), F32), jax.ShapeDtypeStruct((t, d), BF16)],
        compiler_params=_cparams("arbitrary"),
        name="out_proj",
    )(attn, ssd, x, w_attn, w_ssd, n_post.reshape(1, d), gate, n_pre.reshape(1, d), scale, shift)


def _route_kernel(h_ref, wq_ref, sk_ref, a_ref, b_ref, g_ref):
    tt = h_ref.shape[0]
    kd = sk_ref.shape[2]
    q = _dot(h_ref[...], wq_ref[...])
    key_iota = lax.broadcasted_iota(jnp.int32, (PEER_KEYS, tt), 0)
    cand_iota = lax.broadcasted_iota(jnp.int32, (PEER_TOPK * PEER_TOPK, tt), 0)
    a_rows, b_rows, g_rows = [], [], []
    for h in range(PEER_HEADS):
        tops = []
        for s in range(2):
            qhs = q[:, (2 * h + s) * kd:(2 * h + s + 1) * kd]
            sc_t = _dot3(sk_ref[s], qhs, _dot_nt)
            tops.append(_topk_rows(sc_t, PEER_TOPK, key_iota))
        (s1, i1), (s2, i2) = tops
        cand = jnp.concatenate([s1[ka:ka + 1, :] + s2 for ka in range(PEER_TOPK)], axis=0)
        top, pos = _topk_rows(cand, PEER_TOPK, cand_iota)
        pa, pb = pos >> 4, pos & (PEER_TOPK - 1)
        ai = jnp.zeros_like(pos)
        bi = jnp.zeros_like(pos)
        for kk in range(PEER_TOPK):
            ai = jnp.where(pa == kk, i1[kk:kk + 1, :], ai)
            bi = jnp.where(pb == kk, i2[kk:kk + 1, :], bi)
        e = jnp.exp(top - jnp.max(top, axis=0, keepdims=True))
        g_rows.append(e / jnp.sum(e, axis=0, keepdims=True))
        a_rows.append(ai)
        b_rows.append(bi)
    a_ref[...] = jnp.concatenate(a_rows, axis=0).astype(F32).T.astype(jnp.int32)
    b_ref[...] = jnp.concatenate(b_rows, axis=0).astype(F32).T.astype(jnp.int32)
    g_ref[...] = jnp.concatenate(g_rows, axis=0).T


def _peer_route(h2, wq, sub_keys, tt):
    t, d = h2.shape
    nq = wq.shape[1]
    ns = PEER_HEADS * PEER_TOPK
    assert ns == LANES and sub_keys.shape[1] == PEER_KEYS == LANES and PEER_TOPK == 16
    tok = pl.BlockSpec((tt, ns), lambda i: (i, 0))
    return pl.pallas_call(
        _route_kernel,
        grid=(t // tt,),
        in_specs=[pl.BlockSpec((tt, d), lambda i: (i, 0)),
                  pl.BlockSpec((d, nq), lambda i: (0, 0)),
                  pl.BlockSpec(sub_keys.shape, lambda i: (0, 0, 0))],
        out_specs=[tok, tok, tok],
        out_shape=[jax.ShapeDtypeStruct((t, ns), jnp.int32), jax.ShapeDtypeStruct((t, ns), jnp.int32),
                   jax.ShapeDtypeStruct((t, ns), F32)],
        compiler_params=_cparams("arbitrary"),
        name="peer_route",
    )(h2, wq, sub_keys)


def _peer_kernel(h_ref, a_ref, b_ref, g_ref, ulo_ref, uhi_ref, vlo_ref, vhi_ref, o_ref, w_scr, *, na):
    j = pl.program_id(1)
    tt = h_ref.shape[0]
    nk = PEER_KEYS
    half = nk // 2
    hi_mask = jnp.uint32(0xFFFF0000)

    @pl.when(j == 0)
    def _():
        sub = lax.broadcasted_iota(jnp.int32, (nk, LANES), 0)

        def build(tb, carry):
            for r in range(PEER_BUILD_UNROLL):
                t = tb * PEER_BUILD_UNROLL + r
                arow = a_ref[pl.ds(t, 1), :]
                brow = b_ref[pl.ds(t, 1), :]
                grow = g_ref[pl.ds(t, 1), :]
                oa = jnp.where(sub == arow, 1.0, 0.0).astype(BF16)
                zb = jnp.where(sub == brow, grow, 0.0)
                wt = _dot_nt(oa, zb.astype(BF16))
                bits = lax.bitcast_convert_type(wt, jnp.uint32) + jnp.uint32(0x8000)
                w_scr[pl.ds(pl.multiple_of(t * W_PITCH, 8), half), :] = (bits[half:] & hi_mask) | (bits[:half] >> 16)
            return carry

        lax.fori_loop(0, tt // PEER_BUILD_UNROLL, build, 0)

    x = h_ref[...]
    words = [w_scr[pl.ds(j * na + a, tt, stride=W_PITCH), :] for a in range(na)]
    w_lo = jnp.concatenate([lax.bitcast_convert_type(wd << 16, F32) for wd in words], axis=1)
    w_hi = jnp.concatenate([lax.bitcast_convert_type(wd & hi_mask, F32) for wd in words], axis=1)
    p_lo = (w_lo * jax.nn.gelu(_dot_nt(x, ulo_ref[...]))).astype(BF16)
    p_hi = (w_hi * jax.nn.gelu(_dot_nt(x, uhi_ref[...]))).astype(BF16)
    contrib = _dot(p_lo, vlo_ref[...]) + _dot(p_hi, vhi_ref[...])

    @pl.when(j == 0)
    def _():
        o_ref[...] = contrib

    @pl.when(j != 0)
    def _():
        o_ref[...] += contrib


def _peer_experts(h2, a_idx, b_idx, g, u, v, tt, na):
    t, d = h2.shape
    ne = u.shape[0]
    eb = na * PEER_KEYS
    nsteps = ne // (2 * eb)
    assert ne == PEER_KEYS * PEER_KEYS and nsteps * 2 * eb == ne and t % tt == 0 and tt % PEER_BUILD_UNROLL == 0
    ns = a_idx.shape[1]
    tok = pl.BlockSpec((tt, ns), lambda i, j: (i, 0))
    lo = pl.BlockSpec((eb, d), lambda i, j: (j, 0))
    hi = pl.BlockSpec((eb, d), lambda i, j: (j + nsteps, 0))
    return pl.pallas_call(
        functools.partial(_peer_kernel, na=na),
        grid=(t // tt, nsteps),
        in_specs=[pl.BlockSpec((tt, d), lambda i, j: (i, 0)), tok, tok, tok, lo, hi, lo, hi],
        out_specs=pl.BlockSpec((tt, d), lambda i, j: (i, 0)),
        out_shape=jax.ShapeDtypeStruct((t, d), F32),
        scratch_shapes=[pltpu.VMEM((tt * W_PITCH, LANES), jnp.uint32)],
        compiler_params=_cparams("arbitrary", "arbitrary"),
        name="peer_experts",
    )(h2, a_idx, b_idx, g, u, u, v, v)


def _final_kernel(x_ref, f_ref, nw_ref, gate_ref, o_ref):
    o_ref[...] = x_ref[...] + gate_ref[...] * _rms(f_ref[...], nw_ref[...])


def _final(x1, ffn, n_post, gate, tm):
    t, d = x1.shape
    mr = gate.shape[0]
    mod_spec = (pl.BlockSpec((1, d), lambda i: (0, 0)) if mr == 1 else pl.BlockSpec((tm, d), lambda i: (i, 0)))
    return pl.pallas_call(
        _final_kernel,
        grid=(t // tm,),
        in_specs=[pl.BlockSpec((tm, d), lambda i: (i, 0)), pl.BlockSpec((tm, d), lambda i: (i, 0)),
                  pl.BlockSpec((1, d), lambda i: (0, 0)), mod_spec],
        out_specs=pl.BlockSpec((tm, d), lambda i: (i, 0)),
        out_shape=jax.ShapeDtypeStruct((t, d), F32),
        compiler_params=_cparams("arbitrary"),
        name="final_residual",
    )(x1, ffn, n_post.reshape(1, d), gate)


def _tile(n, pref):
    t = min(n, pref)
    while n % t:
        t //= 2
    return t


def kernel(x_prompt, x_sample, c_prompt, c_sample, cache_k, cache_v, page_table, state_conv, state_ssm,
           w_ada, b_ada, norm_mix_pre, norm_mix_post, norm_ffn_pre, norm_ffn_post, w_in, conv_w, conv_b,
           dt_bias, a_log, d_skip, ssd_norm_w, w_out, peer_w_query, peer_sub_keys, peer_u, peer_v):
    bp, lp, d = x_prompt.shape
    bs, ls, _ = x_sample.shape
    assert bp == 1 and lp % MOBA_BLOCK == 0 and ls <= SSD_CHUNK
    attn_w = d // 2
    ssd_w = d - attn_w
    nh_a = attn_w // HEAD_DIM
    nh_s = ssd_w // SSD_HEAD_DIM
    cch = conv_w.shape[1]
    n_main = 3 * attn_w + ssd_w + cch
    n_pool, page, _, _ = cache_k.shape
    tp, ts = bp * lp, bs * ls

    rows = bp + bs
    rpad = -rows % 8
    c_all = jnp.pad(jnp.concatenate([c_prompt, c_sample], axis=0), ((0, rpad), (0, 0)))
    mod = _modulation(c_all, w_ada, b_ada)
    mod_p = [mod[0:bp, k * d:(k + 1) * d] for k in range(6)]
    mod_s = [jnp.repeat(mod[bp:rows, k * d:(k + 1) * d], ls, axis=0) for k in range(6)]

    w_main = w_in[:, :n_main].astype(BF16)
    w_dt = jnp.pad(w_in[:, n_main:], ((0, 0), (0, LANES - nh_s))).astype(BF16)
    w_attn = w_out[:attn_w].astype(BF16)
    w_ssd = w_out[attn_w:].astype(BF16)
    wq = peer_w_query.astype(BF16)
    u_b = peer_u.astype(BF16)
    v_b = peer_v.astype(BF16)

    xp = x_prompt.reshape(tp, d)
    xs = x_sample.reshape(ts, d)
    tn = _tile(n_main, 512)
    proj_p, dt_p = _in_proj(xp, norm_mix_pre, mod_p[1], mod_p[0], w_main, w_dt, _tile(tp, 1024), tn)
    proj_s, dt_s = _in_proj(xs, norm_mix_pre, mod_s[1], mod_s[0], w_main, w_dt, _tile(ts, 256), tn)

    attn_p = _moba_prompt(proj_p, lp, attn_w)
    ck_t = jnp.transpose(cache_k, (0, 2, 3, 1)).reshape(n_pool, attn_w, page)
    cv_t = jnp.transpose(cache_v, (0, 2, 3, 1)).reshape(n_pool, attn_w, page)
    attn_s = _moba_sample(proj_s.reshape(bs, ls, n_main), ck_t, cv_t, page_table, attn_w).reshape(ts, attn_w)

    conv0_p = jnp.zeros((bp, SSD_CONV - 1, cch), F32)
    ssm0_p = jnp.zeros((bp, ssd_w, SSD_STATE), F32)
    ssd_args = (conv_w, conv_b, dt_bias, a_log, d_skip, ssd_norm_w)
    ssd_p, ssm_p = _ssd(proj_p, dt_p, conv0_p, ssm0_p, *ssd_args, bp, lp, ssd_w)
    ssd_s, ssm_s = _ssd(proj_s, dt_s, state_conv, state_ssm.reshape(bs, ssd_w, SSD_STATE), *ssd_args,
                        bs, ls, ssd_w)

    x1_p, h2_p = _out_proj(attn_p, ssd_p, xp, w_attn, w_ssd, norm_mix_post, mod_p[2], norm_ffn_pre,
                           mod_p[4], mod_p[3], _tile(tp, 512))
    x1_s, h2_s = _out_proj(attn_s, ssd_s, xs, w_attn, w_ssd, norm_mix_post, mod_s[2], norm_ffn_pre,
                           mod_s[4], mod_s[3], _tile(ts, 256))

    h2 = jnp.concatenate([h2_p, h2_s], axis=0)
    a_idx, b_idx, g = _peer_route(h2, wq, peer_sub_keys, _tile(tp + ts, 256))
    tt_e = PEER_TILES[0] if (tp + ts) % PEER_TILES[0] == 0 else _tile(tp + ts, 256)
    ffn = _peer_experts(h2, a_idx, b_idx, g, u_b, v_b, tt_e, PEER_TILES[1])

    y_p = _final(x1_p, ffn[:tp], norm_ffn_post, mod_p[5], _tile(tp, 512))
    y_s = _final(x1_s, ffn[tp:], norm_ffn_post, mod_s[5], _tile(ts, 256))

    kcol, vcol, xcol = attn_w, 2 * attn_w, 3 * attn_w + ssd_w
    k_p = proj_p[:, kcol:kcol + attn_w].reshape(bp, lp, nh_a, HEAD_DIM)
    v_p = proj_p[:, vcol:vcol + attn_w].reshape(bp, lp, nh_a, HEAD_DIM)
    conv_p = proj_p[:, xcol:xcol + cch].reshape(bp, lp, cch)[:, lp - (SSD_CONV - 1):]
    k_s = proj_s[:, kcol:kcol + attn_w].reshape(bs, ls, nh_a, HEAD_DIM)
    v_s = proj_s[:, vcol:vcol + attn_w].reshape(bs, ls, nh_a, HEAD_DIM)
    xbc_s = proj_s[:, xcol:xcol + cch].reshape(bs, ls, cch)
    conv_s = jnp.concatenate([state_conv, xbc_s], axis=1)[:, ls:]
    return (y_p.reshape(bp, lp, d), y_s.reshape(bs, ls, d), k_p, v_p, conv_p,
            ssm_p.reshape(bp, nh_s, SSD_HEAD_DIM, SSD_STATE), k_s, v_s, conv_s,
            ssm_s.reshape(bs, nh_s, SSD_HEAD_DIM, SSD_STATE))
```

```python
import functools

import jax
import jax.numpy as jnp
from jax import lax
from jax.experimental import pallas as pl
from jax.experimental.pallas import tpu as pltpu

F32 = jnp.float32
BF16 = jnp.bfloat16

HEAD_DIM = 64
MOBA_BLOCK = 256
MOBA_TOPK = 3
SSD_HEAD_DIM = 64
SSD_GROUPS = 2
SSD_STATE = 128
SSD_CONV = 4
SSD_CHUNK = 128
PEER_HEADS = 8
PEER_KEYS = 128
PEER_TOPK = 16
RMS_EPS = 1e-6

LANES = 128
NEG = -1e30
VMEM_LIMIT = 56 * 1024 * 1024
MOBA_UNROLL = 2
W_PITCH = 136
PEER_BUILD_UNROLL = 8
PEER_TILES = (384, 2)
SAMPLE_BLOCKS = 4


def _cparams(*sem):
    return pltpu.CompilerParams(dimension_semantics=sem, vmem_limit_bytes=VMEM_LIMIT)


def _dot(a, b):
    return jnp.dot(a, b, preferred_element_type=F32)


def _dot_nt(a, b):
    return lax.dot_general(a, b, (((1,), (1,)), ((), ())), preferred_element_type=F32)


def _dot_tn(a, b):
    return lax.dot_general(a, b, (((0,), (0,)), ((), ())), preferred_element_type=F32)


def _split2(x):
    hi = x.astype(BF16)
    lo = (x - hi.astype(F32)).astype(BF16)
    return hi, lo


def _split3(x):
    hi = x.astype(BF16)
    r = x - hi.astype(F32)
    mid = r.astype(BF16)
    lo = (r - mid.astype(F32)).astype(BF16)
    return hi, mid, lo


def _dot3(a, b, dot=_dot):
    ah, al = _split2(a)
    bh, bl = _split2(b)
    return dot(ah, bh) + (dot(ah, bl) + dot(al, bh))


def _dot_exact_rhs(a, b_bf16, dot=_dot):
    h, m, l = _split3(a)
    return dot(h, b_bf16) + (dot(m, b_bf16) + dot(l, b_bf16))


def _silu(x):
    return x * jax.nn.sigmoid(x)


def _rms(x, w):
    return x * lax.rsqrt(jnp.mean(x * x, axis=-1, keepdims=True) + RMS_EPS) * w


def _topk_rows(cur, k, iota):
    vals, idxs = [], []
    for _ in range(k):
        m = jnp.max(cur, axis=0, keepdims=True)
        idx = jnp.min(jnp.where(cur == m, iota, jnp.iinfo(jnp.int32).max), axis=0, keepdims=True)
        vals.append(m)
        idxs.append(idx)
        cur = jnp.where(iota == idx, -jnp.inf, cur)
    return jnp.concatenate(vals, axis=0), jnp.concatenate(idxs, axis=0)


def _mod_kernel(c_ref, w_ref, b_ref, o_ref):
    s = _silu(c_ref[...]).astype(BF16)
    o_ref[...] = _dot(s, w_ref[...].astype(BF16)) + b_ref[...]


def _modulation(c_all, w_ada, b_ada):
    r, d = c_all.shape
    n = w_ada.shape[1]
    tn = 1536 if n % 1536 == 0 else n
    return pl.pallas_call(
        _mod_kernel,
        grid=(n // tn,),
        in_specs=[pl.BlockSpec((r, d), lambda j: (0, 0)),
                  pl.BlockSpec((d, tn), lambda j: (0, j)),
                  pl.BlockSpec((1, tn), lambda j: (0, j))],
        out_specs=pl.BlockSpec((r, tn), lambda j: (0, j)),
        out_shape=jax.ShapeDtypeStruct((r, n), F32),
        compiler_params=_cparams("arbitrary"),
        name="adaln_mod",
    )(c_all, w_ada, b_ada.reshape(1, n))


def _in_kernel(x_ref, nw_ref, sc_ref, sh_ref, w_ref, wdt_ref, o_ref, odt_ref, h_scr):
    @pl.when(pl.program_id(1) == 0)
    def _():
        h = _rms(x_ref[...], nw_ref[...]) * (1.0 + sc_ref[...]) + sh_ref[...]
        h_scr[...] = h.astype(BF16)
        odt_ref[...] = _dot(h_scr[...], wdt_ref[...])

    o_ref[...] = _dot(h_scr[...], w_ref[...])


def _in_proj(x, norm_w, scale, shift, w_main, w_dt, tm, tn):
    t, d = x.shape
    n = w_main.shape[1]
    mr = scale.shape[0]
    mod_spec = (pl.BlockSpec((1, d), lambda i, j: (0, 0)) if mr == 1
                else pl.BlockSpec((tm, d), lambda i, j: (i, 0)))
    return pl.pallas_call(
        _in_kernel,
        grid=(t // tm, n // tn),
        in_specs=[pl.BlockSpec((tm, d), lambda i, j: (i, 0)),
                  pl.BlockSpec((1, d), lambda i, j: (0, 0)),
                  mod_spec, mod_spec,
                  pl.BlockSpec((d, tn), lambda i, j: (0, j)),
                  pl.BlockSpec((d, LANES), lambda i, j: (0, 0))],
        out_specs=[pl.BlockSpec((tm, tn), lambda i, j: (i, j)),
                   pl.BlockSpec((tm, LANES), lambda i, j: (i, 0))],
        out_shape=[jax.ShapeDtypeStruct((t, n), F32),
                   jax.ShapeDtypeStruct((t, LANES), F32)],
        scratch_shapes=[pltpu.VMEM((tm, d), BF16)],
        compiler_params=_cparams("arbitrary", "arbitrary"),
        name="in_proj",
    )(x, norm_w.reshape(1, d), scale, shift, w_main, w_dt)


def _moba_prompt_kernel(q_ref, k_ref, v_ref, o_ref, kb_scr, vt_scr, kmean_scr, sel_scr, sc_scr, pr_scr, *, nb):
    i = pl.program_id(1)
    blk = MOBA_BLOCK
    hd = HEAD_DIM

    @pl.when(i == 0)
    def _():
        row = lax.broadcasted_iota(jnp.int32, (2 * hd, blk), 0)
        ones_row = jnp.where(row == hd, 1.0, 0.0)
        for n in range(nb):
            kn = k_ref[pl.ds(n * blk, blk), :]
            kmean_scr[pl.ds(n, 1), :] = jnp.sum(kn, axis=0, keepdims=True) * (1.0 / blk)
            kb_scr[pl.ds(n * blk, blk), :] = kn.astype(BF16)
            vt = v_ref[pl.ds(n * blk, blk), :].T
            vt_scr[0, :, pl.ds(n * blk, blk)] = jnp.where(row < hd, vt, ones_row).astype(BF16)
            vt_sw = jnp.concatenate([vt[hd:], vt[:hd]], axis=0)
            vt_scr[1, :, pl.ds(n * blk, blk)] = jnp.where(row < hd, vt_sw, ones_row).astype(BF16)

    q = q_ref[...]
    lane_head = lax.broadcasted_iota(jnp.int32, q.shape, 1) // hd
    blk_iota = lax.broadcasted_iota(jnp.int32, (nb, blk), 0)
    key_i = lax.broadcasted_iota(jnp.int32, (blk, blk), 0)
    qry_i = lax.broadcasted_iota(jnp.int32, (blk, blk), 1)
    scale = hd ** -0.5

    qs = []
    for hh in range(2):
        qm = jnp.where(lane_head == hh, q, 0.0)
        s_blk = _dot3(kmean_scr[...], qm, _dot_nt)
        cur = jnp.where(blk_iota < i, s_blk, -jnp.inf)
        sel = jnp.zeros((nb, blk), F32)
        for s in range(min(MOBA_TOPK, nb)):
            m = jnp.max(cur, axis=0, keepdims=True)
            idx = jnp.min(jnp.where(cur == m, blk_iota, nb), axis=0, keepdims=True)
            hit = blk_iota == idx
            sel = jnp.where(hit, jnp.where(i > s, 1.0, sel), sel)
            cur = jnp.where(hit, -jnp.inf, cur)
        sel_scr[hh] = sel
        qs.append((qm * scale).astype(BF16))

    span = MOBA_UNROLL * blk

    def scores(off, width):
        kb = kb_scr[pl.ds(off, width), :]
        return [_dot_nt(kb, qs[hh]) for hh in range(2)]

    def softmax_step(s_t, masks, m_run):
        s_t = jnp.concatenate([jnp.where(mk, s_t[u * blk:(u + 1) * blk], NEG) for u, mk in enumerate(masks)], axis=0)
        m_new = jnp.maximum(m_run, jnp.max(s_t, axis=0, keepdims=True))
        return m_new, jnp.exp(m_run - m_new), jnp.exp(s_t - m_new).astype(BF16)

    def accumulate(hh, off, width, alpha, p, acc):
        return alpha * acc + _dot(vt_scr[hh, :, pl.ds(off, width)], p)

    nsteps = (i + MOBA_UNROLL - 1) // MOBA_UNROLL
    max_step = nb // MOBA_UNROLL - 1
    for hh, s0 in enumerate(scores(0, span)):
        sc_scr[0, hh] = s0
        pr_scr[1, hh] = jnp.zeros((span, blk), BF16)

    def body(k, carry):
        m, acc, alpha_prev = carry
        for cur in range(2):
            nxt = 1 - cur
            t = 2 * k + cur
            live = t < nsteps
            s_next = scores(pl.multiple_of(jnp.minimum(t + 1, max_step) * span, span), span)
            off_prev = pl.multiple_of(jnp.maximum(t - 1, 0) * span, span)
            tm = jnp.minimum(t, max_step)
            m_out, acc_out, alpha_out = [], [], []
            for hh in range(2):
                sc_scr[nxt, hh] = s_next[hh]
                masks = [(sel_scr[hh, pl.ds(tm * MOBA_UNROLL + u, 1), :] > 0.5) & live for u in range(MOBA_UNROLL)]
                m_new, alpha, p = softmax_step(sc_scr[cur, hh], masks, m[hh])
                acc_out.append(accumulate(hh, off_prev, span, alpha_prev[hh], pr_scr[nxt, hh], acc[hh]))
                pr_scr[cur, hh] = p
                m_out.append(m_new)
                alpha_out.append(alpha)
            m, acc, alpha_prev = m_out, acc_out, alpha_out
        return m, acc, alpha_prev

    zero_row = jnp.zeros((1, blk), F32)
    init = ([zero_row + NEG] * 2, [jnp.zeros((2 * hd, blk), F32)] * 2, [zero_row + 1.0] * 2)
    ntrips = (nsteps + 1) // 2
    m, acc, alpha_prev = lax.fori_loop(0, ntrips, body, init)
    off_last = pl.multiple_of(jnp.clip(2 * ntrips - 1, 0, max_step) * span, span)
    off_own = pl.multiple_of(i * blk, blk)
    s_own = scores(off_own, blk)
    outs = []
    for hh in range(2):
        a = accumulate(hh, off_last, span, alpha_prev[hh], pr_scr[1, hh], acc[hh])
        _, alpha, p = softmax_step(s_own[hh], [key_i <= qry_i], m[hh])
        a = accumulate(hh, off_own, blk, alpha, p, a)
        outs.append(a[:hd] / a[hd:hd + 1])
    o_ref[...] = jnp.concatenate(outs, axis=0).T


def _moba_prompt(proj, t, attn_w):
    nb = t // MOBA_BLOCK
    assert nb % MOBA_UNROLL == 0
    npair = attn_w // LANES
    kcol = attn_w // LANES
    return pl.pallas_call(
        functools.partial(_moba_prompt_kernel, nb=nb),
        grid=(npair, nb),
        in_specs=[pl.BlockSpec((MOBA_BLOCK, LANES), lambda p, i: (i, p)),
                  pl.BlockSpec((t, LANES), lambda p, i: (0, kcol + p)),
                  pl.BlockSpec((t, LANES), lambda p, i: (0, 2 * kcol + p))],
        out_specs=pl.BlockSpec((MOBA_BLOCK, LANES), lambda p, i: (i, p)),
        out_shape=jax.ShapeDtypeStruct((t, attn_w), F32),
        scratch_shapes=[pltpu.VMEM((t, LANES), BF16),
                        pltpu.VMEM((2, LANES, t), BF16),
                        pltpu.VMEM((nb, LANES), F32),
                        pltpu.VMEM((2, nb, MOBA_BLOCK), F32),
                        pltpu.VMEM((2, 2, MOBA_UNROLL * MOBA_BLOCK, MOBA_BLOCK), F32),
                        pltpu.VMEM((2, 2, MOBA_UNROLL * MOBA_BLOCK, MOBA_BLOCK), BF16)],
        compiler_params=_cparams("arbitrary", "arbitrary"),
        name="moba_prompt",
    )(proj, proj, proj)


def _moba_sample_kernel(pt_ref, q_ref, kn_ref, vn_ref, *rest, nb, nh, lq, pb):
    npg = 2 * pb
    k_refs, v_refs = rest[:npg], rest[npg:2 * npg]
    o_ref, q2_scr, s_scr, mean_scr, max_scr, sel_scr, own_scr, m_scr, l_scr, acc_scr = rest[2 * npg:]
    ph = pl.program_id(1)
    n = pl.program_id(2)
    last = pl.num_programs(2) - 1
    r = nh * lq
    w = nh * HEAD_DIM
    blk = MOBA_BLOCK
    scale = HEAD_DIM ** -0.5
    row_head = lax.broadcasted_iota(jnp.int32, (r, w), 0) // lq
    col_head = lax.broadcasted_iota(jnp.int32, (r, w), 1) // HEAD_DIM
    lane = lax.broadcasted_iota(jnp.int32, (r, LANES), 1)
    own_ok = (lane < lq) & (lane <= lax.broadcasted_iota(jnp.int32, (r, LANES), 0) % lq)

    def block_t(refs, u):
        return jnp.concatenate([refs[2 * u][0], refs[2 * u + 1][0]], axis=1).astype(BF16)

    @pl.when((ph == 0) & (n == 0))
    def _():
        q = q_ref[0]
        qbd = jnp.where(row_head == col_head, jnp.concatenate([q] * nh, axis=0), 0.0)
        hi, lo = _split2(qbd)
        q2_scr[...] = jnp.concatenate([hi, lo], axis=0)
        mean_scr[...] = jnp.zeros(mean_scr.shape, F32)
        max_scr[...] = jnp.zeros(max_scr.shape, F32)

    @pl.when(ph == 0)
    def _():
        means, maxes = mean_scr[...], max_scr[...]
        for u in range(pb):
            s2 = _dot(q2_scr[...], block_t(k_refs, u))
            s = s2[:r] + s2[r:]
            nu = n * pb + u
            s_scr[:, pl.ds(pl.multiple_of(nu * blk, blk), blk)] = s
            means = jnp.where(lane == nu, jnp.sum(s, axis=1, keepdims=True) * (1.0 / blk), means)
            maxes = jnp.where(lane == nu, jnp.max(s, axis=1, keepdims=True), maxes)
        mean_scr[...] = means
        max_scr[...] = maxes

    @pl.when((ph == 0) & (n == last))
    def _():
        cur = jnp.where(lane < nb, mean_scr[...], -jnp.inf)
        sel = jnp.zeros((r, LANES), F32)
        for _ in range(MOBA_TOPK):
            mx = jnp.max(cur, axis=1, keepdims=True)
            idx = jnp.min(jnp.where(cur == mx, lane, LANES), axis=1, keepdims=True)
            hit = lane == idx
            sel = jnp.where(hit, 1.0, sel)
            cur = jnp.where(hit, -jnp.inf, cur)
        sel_scr[...] = sel
        kn = jnp.concatenate([kn_ref[0], jnp.zeros((LANES - lq, w), F32)], axis=0).astype(BF16)
        o2 = _dot_nt(q2_scr[...], kn)
        own = jnp.where(own_ok, o2[:r] + o2[r:], NEG)
        own_scr[...] = own
        m_sel = jnp.max(jnp.where(sel > 0.5, max_scr[...], NEG), axis=1, keepdims=True)
        m_scr[...] = jnp.maximum(m_sel, jnp.max(own, axis=1, keepdims=True))
        l_scr[...] = jnp.zeros(l_scr.shape, F32)
        acc_scr[...] = jnp.zeros(acc_scr.shape, F32)

    @pl.when(ph == 1)
    def _():
        l_add = jnp.zeros((r, 1), F32)
        acc_add = jnp.zeros((r, w), F32)
        for u in range(pb):
            nu = n * pb + u
            picked = jnp.sum(jnp.where(lane == nu, sel_scr[...], 0.0), axis=1, keepdims=True) > 0.5
            s = s_scr[:, pl.ds(pl.multiple_of(nu * blk, blk), blk)]
            p = jnp.where(picked, jnp.exp((s - m_scr[...]) * scale), 0.0)
            l_add = l_add + jnp.sum(p, axis=1, keepdims=True)
            acc_add = acc_add + _dot_nt(p.astype(BF16), block_t(v_refs, u))
        l_scr[...] += l_add
        acc_scr[...] += acc_add

    @pl.when((ph == 1) & (n == last))
    def _():
        p = jnp.where(own_ok, jnp.exp((own_scr[...] - m_scr[...]) * scale), 0.0)
        vn = jnp.concatenate([vn_ref[0], jnp.zeros((LANES - lq, w), F32)], axis=0).astype(BF16)
        l_fin = l_scr[...] + jnp.sum(p, axis=1, keepdims=True)
        acc = acc_scr[...] + _dot(p.astype(BF16), vn)
        full = jnp.where(row_head == col_head, acc / l_fin, 0.0)
        out = full[0:lq]
        for h in range(1, nh):
            out = out + full[h * lq:(h + 1) * lq]
        o_ref[0] = out


def _moba_sample(proj3, cache_kt, cache_vt, page_table, attn_w):
    b, lq, _ = proj3.shape
    _, w, page = cache_kt.shape
    n_pages = page_table.shape[1]
    nh = attn_w // HEAD_DIM
    r = nh * lq
    assert 2 * page == MOBA_BLOCK and page == LANES and w == attn_w and lq <= LANES and r % 8 == 0
    nb = n_pages // 2
    assert nb * 2 == n_pages and MOBA_TOPK <= nb <= LANES
    pb = SAMPLE_BLOCKS if nb % SAMPLE_BLOCKS == 0 else 1
    nsteps = nb // pb
    npg = 2 * pb
    kmap = lambda o: (lambda bi, ph, n, pt: (pt[bi, npg * (n * (1 - ph) + (nsteps - 1) * ph) + o], 0, 0))
    vmap = lambda o: (lambda bi, ph, n, pt: (pt[bi, npg * (n * ph) + o], 0, 0))
    newspec = lambda c: pl.BlockSpec((1, lq, attn_w), lambda bi, ph, n, pt: (bi, 0, c))
    pages = ([pl.BlockSpec((1, w, page), kmap(o)) for o in range(npg)]
             + [pl.BlockSpec((1, w, page), vmap(o)) for o in range(npg)])
    return pl.pallas_call(
        functools.partial(_moba_sample_kernel, nb=nb, nh=nh, lq=lq, pb=pb),
        grid_spec=pltpu.PrefetchScalarGridSpec(
            num_scalar_prefetch=1,
            grid=(b, 2, nsteps),
            in_specs=[newspec(0), newspec(1), newspec(2)] + pages,
            out_specs=pl.BlockSpec((1, lq, attn_w), lambda bi, ph, n, pt: (bi, 0, 0)),
            scratch_shapes=[pltpu.VMEM((2 * r, w), BF16),
                            pltpu.VMEM((r, nb * MOBA_BLOCK), F32),
                            pltpu.VMEM((r, LANES), F32),
                            pltpu.VMEM((r, LANES), F32),
                            pltpu.VMEM((r, LANES), F32),
                            pltpu.VMEM((r, LANES), F32),
                            pltpu.VMEM((r, 1), F32),
                            pltpu.VMEM((r, 1), F32),
                            pltpu.VMEM((r, w), F32)]),
        out_shape=jax.ShapeDtypeStruct((b, lq, attn_w), F32),
        compiler_params=_cparams("arbitrary", "arbitrary", "arbitrary"),
        name="moba_sample",
    )(page_table, proj3, proj3, proj3, *([cache_kt] * npg), *([cache_vt] * npg))


def _ssd_kernel(z_ref, xs_ref, bm_ref, cm_ref, dt_ref, conv0_ref, ssm0_ref,
                cw_ref, cb_ref, dtb_ref, alog_ref, dskip_ref, nw_ref, exp_h_ref, exp_r_ref,
                y_ref, ssm_ref, bx_scr, bb_scr, bc_scr, st_scr, *, lv, nheads):
    c = pl.program_id(1)
    q = SSD_CHUNK
    wx = xs_ref.shape[-1]
    ws = bm_ref.shape[-1]
    pad = 8
    npair = wx // LANES
    rep = nheads // SSD_GROUPS

    @pl.when(c == 0)
    def _():
        bx_scr[...] = jnp.zeros(bx_scr.shape, F32)
        bb_scr[...] = jnp.zeros(bb_scr.shape, F32)
        bc_scr[...] = jnp.zeros(bc_scr.shape, F32)
        c0 = conv0_ref[0]
        lo = pad - (SSD_CONV - 1)
        bx_scr[lo:pad, :] = c0[:, :wx]
        bb_scr[lo:pad, :] = c0[:, wx:wx + ws]
        bc_scr[lo:pad, :] = c0[:, wx + ws:]
        st_scr[...] = ssm0_ref[0]

    def conv(src_ref, buf, c_lo, c_hi):
        buf[pad:pad + lv, :] = src_ref[...]
        acc = cb_ref[:, c_lo:c_hi]
        for i in range(SSD_CONV):
            acc = acc + buf[pl.ds(pad - (SSD_CONV - 1) + i, q), :] * cw_ref[i:i + 1, c_lo:c_hi]
        buf[pad - (SSD_CONV - 1):pad, :] = buf[pad + lv - (SSD_CONV - 1):pad + lv, :]
        return _silu(acc)

    xs = conv(xs_ref, bx_scr, 0, wx)
    bm = conv(bm_ref, bb_scr, wx, wx + ws)
    cm = conv(cm_ref, bc_scr, wx + ws, wx + 2 * ws)

    row = lax.broadcasted_iota(jnp.int32, (q, LANES), 0)
    if lv == q:
        dt_raw = dt_ref[...]
    else:
        dt_raw = jnp.concatenate([dt_ref[...], jnp.zeros((q - lv, LANES), F32)], axis=0)
    dt = jnp.where(row < lv, jax.nn.softplus(dt_raw + dtb_ref[...]), 0.0)
    da = dt * (-jnp.exp(alog_ref[...]))

    li = lax.broadcasted_iota(jnp.int32, (q, q), 0)
    si = lax.broadcasted_iota(jnp.int32, (q, q), 1)
    tril = li >= si
    tri_l = jnp.where(tril, 1.0, 0.0).astype(BF16)
    tri_u = jnp.where(li <= si, 1.0, 0.0).astype(BF16)
    cs = _dot_exact_rhs(da, tri_l, lambda a, t: _dot(t, a))
    cs_t = _dot_exact_rhs(da, tri_u, _dot_tn)
    exp_h = exp_h_ref[...]
    dt_full = _dot_exact_rhs(dt, exp_h)
    cs_full = _dot_exact_rhs(cs, exp_h)
    cs_rep = _dot_exact_rhs(cs, exp_r_ref[...])
    cs_last = cs_full[q - 1:q, :]
    xdt = xs * dt_full
    xdec = (xdt * jnp.exp(cs_last - cs_full)).astype(BF16)
    exp_cs = jnp.exp(cs_full)
    xdt_b = xdt.astype(BF16)
    lane_half = lax.broadcasted_iota(jnp.int32, (q, LANES), 1) // SSD_HEAD_DIM
    rowh = lax.broadcasted_iota(jnp.int32, (LANES, 1), 0) // SSD_HEAD_DIM
    last_t = cs_t[:, q - 1:q]

    g_scores = []
    for g in range(SSD_GROUPS):
        cg = cm[:, g * SSD_STATE:(g + 1) * SSD_STATE].astype(BF16)
        bg = bm[:, g * SSD_STATE:(g + 1) * SSD_STATE].astype(BF16)
        g_scores.append((cg, bg, _dot_nt(cg, bg)))

    ys = []
    for p in range(npair):
        h0 = 2 * p
        cg, bg, sc = g_scores[h0 // rep]
        sl = slice(p * LANES, (p + 1) * LANES)
        xp = xdt_b[:, sl]
        yd = jnp.zeros((q, LANES), F32)
        for hh in range(2):
            h = h0 + hh
            diff = cs_rep[:, h * LANES:h * LANES + q] - cs_t[h:h + 1, :]
            lmat = jnp.exp(jnp.where(tril, diff, NEG))
            mh = (sc * lmat).astype(BF16)
            yd = yd + _dot(mh, jnp.where(lane_half == hh, xp, jnp.zeros_like(xp)))
        st = st_scr[sl, :]
        y_off = _dot_nt(cg, st.astype(BF16)) * exp_cs[:, sl]
        ys.append(yd + y_off)
        dec_col = jnp.where(rowh == 0, jnp.exp(last_t[h0:h0 + 1, :]), jnp.exp(last_t[h0 + 1:h0 + 2, :]))
        st_scr[sl, :] = dec_col * st + _dot_tn(xdec[:, sl], bg)

    y = jnp.concatenate(ys, axis=1) + xs * dskip_ref[...]
    y_ref[...] = _rms(y[:lv] * _silu(z_ref[...]), nw_ref[...])
    ssm_ref[0] = st_scr[...]


def _ssd(proj, proj_dt, conv0, ssm0, conv_w, conv_b, dt_bias, a_log, d_skip, norm_w, b, l, ssd_w):
    ws = SSD_GROUPS * SSD_STATE
    nheads = ssd_w // SSD_HEAD_DIM
    assert nheads <= LANES and nheads % (2 * SSD_GROUPS) == 0 and SSD_STATE == LANES
    lv = min(l, SSD_CHUNK)
    nc = l // lv
    assert nc * lv == l
    attn_cols = proj.shape[1] - 2 * ssd_w - 2 * ws
    assert attn_cols % ssd_w == 0 and (attn_cols + 2 * ssd_w) % ws == 0
    zc = attn_cols // ssd_w
    xc = zc + 1
    bc = (attn_cols + 2 * ssd_w) // ws
    hpad = lambda v: jnp.pad(v.reshape(1, -1), ((0, 0), (0, LANES - nheads)))
    hid = jnp.arange(LANES)[:, None]
    exp_h = (hid == (jnp.arange(ssd_w)[None, :] // SSD_HEAD_DIM)).astype(BF16)
    exp_r = (hid == (jnp.arange(nheads * LANES)[None, :] // LANES)).astype(BF16)
    cch = conv_w.shape[1]
    rowblk = lambda bi, c: bi * nc + c
    const = lambda bi, c: (0, 0)
    return pl.pallas_call(
        functools.partial(_ssd_kernel, lv=lv, nheads=nheads),
        grid=(b, nc),
        in_specs=[pl.BlockSpec((lv, ssd_w), lambda bi, c: (rowblk(bi, c), zc)),
                  pl.BlockSpec((lv, ssd_w), lambda bi, c: (rowblk(bi, c), xc)),
                  pl.BlockSpec((lv, ws), lambda bi, c: (rowblk(bi, c), bc)),
                  pl.BlockSpec((lv, ws), lambda bi, c: (rowblk(bi, c), bc + 1)),
                  pl.BlockSpec((lv, LANES), lambda bi, c: (rowblk(bi, c), 0)),
                  pl.BlockSpec((1, SSD_CONV - 1, cch), lambda bi, c: (bi, 0, 0)),
                  pl.BlockSpec((1, ssd_w, SSD_STATE), lambda bi, c: (bi, 0, 0)),
                  pl.BlockSpec((SSD_CONV, cch), const),
                  pl.BlockSpec((1, cch), const),
                  pl.BlockSpec((1, LANES), const),
                  pl.BlockSpec((1, LANES), const),
                  pl.BlockSpec((1, ssd_w), const),
                  pl.BlockSpec((1, ssd_w), const),
                  pl.BlockSpec((LANES, ssd_w), const),
                  pl.BlockSpec((LANES, nheads * LANES), const)],
        out_specs=[pl.BlockSpec((lv, ssd_w), lambda bi, c: (rowblk(bi, c), 0)),
                   pl.BlockSpec((1, ssd_w, SSD_STATE), lambda bi, c: (bi, 0, 0))],
        out_shape=[jax.ShapeDtypeStruct((b * l, ssd_w), F32),
                   jax.ShapeDtypeStruct((b, ssd_w, SSD_STATE), F32)],
        scratch_shapes=[pltpu.VMEM((SSD_CHUNK + 8, ssd_w), F32),
                        pltpu.VMEM((SSD_CHUNK + 8, ws), F32),
                        pltpu.VMEM((SSD_CHUNK + 8, ws), F32),
                        pltpu.VMEM((ssd_w, SSD_STATE), F32)],
        compiler_params=_cparams("arbitrary", "arbitrary"),
        name="ssd_mixer",
    )(proj, proj, proj, proj, proj_dt, conv0, ssm0, conv_w, conv_b.reshape(1, cch),
      hpad(dt_bias), hpad(a_log), jnp.repeat(d_skip, SSD_HEAD_DIM).reshape(1, ssd_w),
      norm_w.reshape(1, ssd_w), exp_h, exp_r)


def _out_kernel(attn_ref, ssd_ref, x_ref, wa_ref, ws_ref, npost_ref, gate_ref, npre_ref, sc_ref, sh_ref,
                x1_ref, h2_ref):
    mix = _dot(attn_ref[...].astype(BF16), wa_ref[...]) + _dot(ssd_ref[...].astype(BF16), ws_ref[...])
    x1 = x_ref[...] + gate_ref[...] * _rms(mix, npost_ref[...])
    x1_ref[...] = x1
    h2_ref[...] = (_rms(x1, npre_ref[...]) * (1.0 + sc_ref[...]) + sh_ref[...]).astype(BF16)


def _out_proj(attn, ssd, x, w_attn, w_ssd, n_post, gate, n_pre, scale, shift, tm):
    t, d = x.shape
    aw, sw = attn.shape[1], ssd.shape[1]
    mr = gate.shape[0]
    mod_spec = (pl.BlockSpec((1, d), lambda i: (0, 0)) if mr == 1 else pl.BlockSpec((tm, d), lambda i: (i, 0)))
    vec = pl.BlockSpec((1, d), lambda i: (0, 0))
    return pl.pallas_call(
        _out_kernel,
        grid=(t // tm,),
        in_specs=[pl.BlockSpec((tm, aw), lambda i: (i, 0)),
                  pl.BlockSpec((tm, sw), lambda i: (i, 0)),
                  pl.BlockSpec((tm, d), lambda i: (i, 0)),
                  pl.BlockSpec((aw, d), lambda i: (0, 0)),
                  pl.BlockSpec((sw, d), lambda i: (0, 0)),
                  vec, mod_spec, vec, mod_spec, mod_spec],
        out_specs=[pl.BlockSpec((tm, d), lambda i: (i, 0)),
                   pl.BlockSpec((tm, d), lambda i: (i, 0))],
        out_shape=[jax.ShapeDtypeStruct((t, d), F32), jax.ShapeDtypeStruct((t, d), BF16)],
        compiler_params=_cparams("arbitrary"),
        name="out_proj",
    )(attn, ssd, x, w_attn, w_ssd, n_post.reshape(1, d), gate, n_pre.reshape(1, d), scale, shift)


def _route_kernel(h_ref, wq_ref, sk_ref, a_ref, b_ref, g_ref):
    tt = h_ref.shape[0]
    kd = sk_ref.shape[2]
    q = _dot(h_ref[...], wq_ref[...])
    key_iota = lax.broadcasted_iota(jnp.int32, (PEER_KEYS, tt), 0)
    k8 = PEER_TOPK // 2
    row8 = lax.broadcasted_iota(jnp.int32, (k8, tt), 0)
    row16 = lax.broadcasted_iota(jnp.int32, (PEER_TOPK, tt), 0)
    flat = [row16 * PEER_TOPK] + [row8 * PEER_TOPK + kb for kb in range(1, k8)] + [row8 + k8]
    cand_flat = jnp.concatenate(flat, axis=0)
    a_rows, b_rows, g_rows = [], [], []
    for h in range(PEER_HEADS):
        tops = []
        for s in range(2):
            qhs = q[:, (2 * h + s) * kd:(2 * h + s + 1) * kd]
            sc_t = _dot3(sk_ref[s], qhs, _dot_nt)
            tops.append(_topk_rows(sc_t, PEER_TOPK, key_iota))
        (s1, i1), (s2, i2) = tops
        pieces = [s1 + s2[0:1, :]]
        for kb in range(1, k8):
            pieces.append(jnp.where(row8 < PEER_TOPK // (kb + 1), s1[0:k8, :] + s2[kb:kb + 1, :], -jnp.inf))
        pieces.append(s1[0:1, :] + s2[k8:, :])
        top, pos = _topk_rows(jnp.concatenate(pieces, axis=0), PEER_TOPK, cand_flat)
        pa, pb = pos >> 4, pos & (PEER_TOPK - 1)
        ai = jnp.zeros_like(pos)
        bi = jnp.zeros_like(pos)
        for kk in range(PEER_TOPK):
            ai = jnp.where(pa == kk, i1[kk:kk + 1, :], ai)
            bi = jnp.where(pb == kk, i2[kk:kk + 1, :], bi)
        e = jnp.exp(top - jnp.max(top, axis=0, keepdims=True))
        g_rows.append(e / jnp.sum(e, axis=0, keepdims=True))
        a_rows.append(ai)
        b_rows.append(bi)
    a_ref[...] = jnp.concatenate(a_rows, axis=0).astype(F32).T.astype(jnp.int32)
    b_ref[...] = jnp.concatenate(b_rows, axis=0).astype(F32).T.astype(jnp.int32)
    g_ref[...] = jnp.concatenate(g_rows, axis=0).T


def _peer_route(h2, wq, sub_keys, tt):
    t, d = h2.shape
    nq = wq.shape[1]
    ns = PEER_HEADS * PEER_TOPK
    assert ns == LANES and sub_keys.shape[1] == PEER_KEYS == LANES and PEER_TOPK == 16
    tok = pl.BlockSpec((tt, ns), lambda i: (i, 0))
    return pl.pallas_call(
        _route_kernel,
        grid=(t // tt,),
        in_specs=[pl.BlockSpec((tt, d), lambda i: (i, 0)),
                  pl.BlockSpec((d, nq), lambda i: (0, 0)),
                  pl.BlockSpec(sub_keys.shape, lambda i: (0, 0, 0))],
        out_specs=[tok, tok, tok],
        out_shape=[jax.ShapeDtypeStruct((t, ns), jnp.int32), jax.ShapeDtypeStruct((t, ns), jnp.int32),
                   jax.ShapeDtypeStruct((t, ns), F32)],
        compiler_params=_cparams("arbitrary"),
        name="peer_route",
    )(h2, wq, sub_keys)


def _peer_kernel(h_ref, a_ref, b_ref, g_ref, ulo_ref, uhi_ref, vlo_ref, vhi_ref, o_ref, w_scr, *, na):
    j = pl.program_id(1)
    tt = h_ref.shape[0]
    nk = PEER_KEYS
    half = nk // 2

    @pl.when(j == 0)
    def _():
        sub = lax.broadcasted_iota(jnp.int32, (nk, LANES), 0)

        def build(tb, carry):
            for r in range(PEER_BUILD_UNROLL):
                t = tb * PEER_BUILD_UNROLL + r
                arow = a_ref[pl.ds(t, 1), :]
                brow = b_ref[pl.ds(t, 1), :]
                grow = g_ref[pl.ds(t, 1), :]
                oa = jnp.where(sub == arow, 1.0, 0.0).astype(BF16)
                zb = jnp.where(sub == brow, grow, 0.0).astype(BF16)
                w_scr[pl.ds(pl.multiple_of(t * W_PITCH, 8), nk), :] = _dot_nt(oa, zb)
            return carry

        lax.fori_loop(0, tt // PEER_BUILD_UNROLL, build, 0)

    x = h_ref[...]

    def weights(a0):
        return jnp.concatenate([w_scr[pl.ds(a0 + a, tt, stride=W_PITCH), :] for a in range(na)], axis=1)

    p_lo = (weights(j * na) * jax.nn.gelu(_dot_nt(x, ulo_ref[...]))).astype(BF16)
    p_hi = (weights(j * na + half) * jax.nn.gelu(_dot_nt(x, uhi_ref[...]))).astype(BF16)
    contrib = _dot(p_lo, vlo_ref[...]) + _dot(p_hi, vhi_ref[...])

    @pl.when(j == 0)
    def _():
        o_ref[...] = contrib

    @pl.when(j != 0)
    def _():
        o_ref[...] += contrib


def _peer_experts(h2, a_idx, b_idx, g, u, v, tt, na):
    t, d = h2.shape
    ne = u.shape[0]
    eb = na * PEER_KEYS
    nsteps = ne // (2 * eb)
    assert ne == PEER_KEYS * PEER_KEYS and nsteps * 2 * eb == ne and t % tt == 0 and tt % PEER_BUILD_UNROLL == 0
    ns = a_idx.shape[1]
    tok = pl.BlockSpec((tt, ns), lambda i, j: (i, 0))
    lo = pl.BlockSpec((eb, d), lambda i, j: (j, 0))
    hi = pl.BlockSpec((eb, d), lambda i, j: (j + nsteps, 0))
    return pl.pallas_call(
        functools.partial(_peer_kernel, na=na),
        grid=(t // tt, nsteps),
        in_specs=[pl.BlockSpec((tt, d), lambda i, j: (i, 0)), tok, tok, tok, lo, hi, lo, hi],
        out_specs=pl.BlockSpec((tt, d), lambda i, j: (i, 0)),
        out_shape=jax.ShapeDtypeStruct((t, d), F32),
        scratch_shapes=[pltpu.VMEM((tt * W_PITCH, LANES), F32)],
        compiler_params=_cparams("arbitrary", "arbitrary"),
        name="peer_experts",
    )(h2, a_idx, b_idx, g, u, u, v, v)


def _final_kernel(x_ref, f_ref, nw_ref, gate_ref, o_ref):
    o_ref[...] = x_ref[...] + gate_ref[...] * _rms(f_ref[...], nw_ref[...])


def _final(x1, ffn, n_post, gate, tm):
    t, d = x1.shape
    mr = gate.shape[0]
    mod_spec = (pl.BlockSpec((1, d), lambda i: (0, 0)) if mr == 1 else pl.BlockSpec((tm, d), lambda i: (i, 0)))
    return pl.pallas_call(
        _final_kernel,
        grid=(t // tm,),
        in_specs=[pl.BlockSpec((tm, d), lambda i: (i, 0)), pl.BlockSpec((tm, d), lambda i: (i, 0)),
                  pl.BlockSpec((1, d), lambda i: (0, 0)), mod_spec],
        out_specs=pl.BlockSpec((tm, d), lambda i: (i, 0)),
        out_shape=jax.ShapeDtypeStruct((t, d), F32),
        compiler_params=_cparams("arbitrary"),
        name="final_residual",
    )(x1, ffn, n_post.reshape(1, d), gate)


def _tile(n, pref):
    t = min(n, pref)
    while n % t:
        t //= 2
    return t


def kernel(x_prompt, x_sample, c_prompt, c_sample, cache_k, cache_v, page_table, state_conv, state_ssm,
           w_ada, b_ada, norm_mix_pre, norm_mix_post, norm_ffn_pre, norm_ffn_post, w_in, conv_w, conv_b,
           dt_bias, a_log, d_skip, ssd_norm_w, w_out, peer_w_query, peer_sub_keys, peer_u, peer_v):
    bp, lp, d = x_prompt.shape
    bs, ls, _ = x_sample.shape
    assert bp == 1 and lp % MOBA_BLOCK == 0 and ls <= SSD_CHUNK
    attn_w = d // 2
    ssd_w = d - attn_w
    nh_a = attn_w // HEAD_DIM
    nh_s = ssd_w // SSD_HEAD_DIM
    cch = conv_w.shape[1]
    n_main = 3 * attn_w + ssd_w + cch
    n_pool, page, _, _ = cache_k.shape
    tp, ts = bp * lp, bs * ls

    rows = bp + bs
    rpad = -rows % 8
    c_all = jnp.pad(jnp.concatenate([c_prompt, c_sample], axis=0), ((0, rpad), (0, 0)))
    mod = _modulation(c_all, w_ada, b_ada)
    mod_p = [mod[0:bp, k * d:(k + 1) * d] for k in range(6)]
    mod_s = [jnp.repeat(mod[bp:rows, k * d:(k + 1) * d], ls, axis=0) for k in range(6)]

    w_main = w_in[:, :n_main].astype(BF16)
    w_dt = jnp.pad(w_in[:, n_main:], ((0, 0), (0, LANES - nh_s))).astype(BF16)
    w_attn = w_out[:attn_w].astype(BF16)
    w_ssd = w_out[attn_w:].astype(BF16)
    wq = peer_w_query.astype(BF16)
    u_b = peer_u.astype(BF16)
    v_b = peer_v.astype(BF16)

    xp = x_prompt.reshape(tp, d)
    xs = x_sample.reshape(ts, d)
    tn = _tile(n_main, 512)
    proj_p, dt_p = _in_proj(xp, norm_mix_pre, mod_p[1], mod_p[0], w_main, w_dt, _tile(tp, 1024), tn)
    proj_s, dt_s = _in_proj(xs, norm_mix_pre, mod_s[1], mod_s[0], w_main, w_dt, _tile(ts, 256), tn)

    attn_p = _moba_prompt(proj_p, lp, attn_w)
    ck_t = jnp.transpose(cache_k, (0, 2, 3, 1)).reshape(n_pool, attn_w, page)
    cv_t = jnp.transpose(cache_v, (0, 2, 3, 1)).reshape(n_pool, attn_w, page)
    attn_s = _moba_sample(proj_s.reshape(bs, ls, n_main), ck_t, cv_t, page_table, attn_w).reshape(ts, attn_w)

    conv0_p = jnp.zeros((bp, SSD_CONV - 1, cch), F32)
    ssm0_p = jnp.zeros((bp, ssd_w, SSD_STATE), F32)
    ssd_args = (conv_w, conv_b, dt_bias, a_log, d_skip, ssd_norm_w)
    ssd_p, ssm_p = _ssd(proj_p, dt_p, conv0_p, ssm0_p, *ssd_args, bp, lp, ssd_w)
    ssd_s, ssm_s = _ssd(proj_s, dt_s, state_conv, state_ssm.reshape(bs, ssd_w, SSD_STATE), *ssd_args,
                        bs, ls, ssd_w)

    x1_p, h2_p = _out_proj(attn_p, ssd_p, xp, w_attn, w_ssd, norm_mix_post, mod_p[2], norm_ffn_pre,
                           mod_p[4], mod_p[3], _tile(tp, 512))
    x1_s, h2_s = _out_proj(attn_s, ssd_s, xs, w_attn, w_ssd, norm_mix_post, mod_s[2], norm_ffn_pre,
                           mod_s[4], mod_s[3], _tile(ts, 256))

    h2 = jnp.concatenate([h2_p, h2_s], axis=0)
    a_idx, b_idx, g = _peer_route(h2, wq, peer_sub_keys, _tile(tp + ts, 256))
    tt_e = PEER_TILES[0] if (tp + ts) % PEER_TILES[0] == 0 else _tile(tp + ts, 256)
    ffn = _peer_experts(h2, a_idx, b_idx, g, u_b, v_b, tt_e, PEER_TILES[1])

    y_p = _final(x1_p, ffn[:tp], norm_ffn_post, mod_p[5], _tile(tp, 512))
    y_s = _final(x1_s, ffn[tp:], norm_ffn_post, mod_s[5], _tile(ts, 256))

    kcol, vcol, xcol = attn_w, 2 * attn_w, 3 * attn_w + ssd_w
    k_p = proj_p[:, kcol:kcol + attn_w].reshape(bp, lp, nh_a, HEAD_DIM)
    v_p = proj_p[:, vcol:vcol + attn_w].reshape(bp, lp, nh_a, HEAD_DIM)
    conv_p = proj_p[:, xcol:xcol + cch].reshape(bp, lp, cch)[:, lp - (SSD_CONV - 1):]
    k_s = proj_s[:, kcol:kcol + attn_w].reshape(bs, ls, nh_a, HEAD_DIM)
    v_s = proj_s[:, vcol:vcol + attn_w].reshape(bs, ls, nh_a, HEAD_DIM)
    xbc_s = proj_s[:, xcol:xcol + cch].reshape(bs, ls, cch)
    conv_s = jnp.concatenate([state_conv, xbc_s], axis=1)[:, ls:]
    return (y_p.reshape(bp, lp, d), y_s.reshape(bs, ls, d), k_p, v_p, conv_p,
            ssm_p.reshape(bp, nh_s, SSD_HEAD_DIM, SSD_STATE), k_s, v_s, conv_s,
            ssm_s.reshape(bs, nh_s, SSD_HEAD_DIM, SSD_STATE))
```

```python
import functools

import jax
import jax.numpy as jnp
from jax import lax
from jax.experimental import pallas as pl
from jax.experimental.pallas import tpu as pltpu

F32 = jnp.float32
BF16 = jnp.bfloat16

HEAD_DIM = 64
MOBA_BLOCK = 256
MOBA_TOPK = 3
SSD_HEAD_DIM = 64
SSD_GROUPS = 2
SSD_STATE = 128
SSD_CONV = 4
SSD_CHUNK = 128
PEER_HEADS = 8
PEER_KEYS = 128
PEER_TOPK = 16
RMS_EPS = 1e-6

LANES = 128
NEG = -1e30
LOG2E = 1.4426950408889634
VMEM_LIMIT = 56 * 1024 * 1024
MOBA_UNROLL = 2
MOBA_QBLOCKS = 2
W_PITCH = 136
PEER_BUILD_UNROLL = 16
PEER_TILES = (384, 4)
SAMPLE_BLOCKS = 4


def _cparams(*sem):
    return pltpu.CompilerParams(dimension_semantics=sem, vmem_limit_bytes=VMEM_LIMIT)


def _dot(a, b):
    return jnp.dot(a, b, preferred_element_type=F32)


def _dot_nt(a, b):
    return lax.dot_general(a, b, (((1,), (1,)), ((), ())), preferred_element_type=F32)


def _dot_tn(a, b):
    return lax.dot_general(a, b, (((0,), (0,)), ((), ())), preferred_element_type=F32)


def _split2(x):
    hi = x.astype(BF16)
    lo = (x - hi.astype(F32)).astype(BF16)
    return hi, lo


def _split3(x):
    hi = x.astype(BF16)
    r = x - hi.astype(F32)
    mid = r.astype(BF16)
    lo = (r - mid.astype(F32)).astype(BF16)
    return hi, mid, lo


def _dot3(a, b, dot=_dot):
    ah, al = _split2(a)
    bh, bl = _split2(b)
    return dot(ah, bh) + (dot(ah, bl) + dot(al, bh))


def _dot_exact_rhs(a, b_bf16, dot=_dot):
    h, m, l = _split3(a)
    return dot(h, b_bf16) + (dot(m, b_bf16) + dot(l, b_bf16))


def _silu(x):
    return x * jax.nn.sigmoid(x)


def _rms(x, w):
    return x * lax.rsqrt(jnp.mean(x * x, axis=-1, keepdims=True) + RMS_EPS) * w


def _topk_rows(cur, k, iota):
    vals, idxs = [], []
    for _ in range(k):
        m = jnp.max(cur, axis=0, keepdims=True)
        idx = jnp.min(jnp.where(cur == m, iota, jnp.iinfo(jnp.int32).max), axis=0, keepdims=True)
        vals.append(m)
        idxs.append(idx)
        cur = jnp.where(iota == idx, -jnp.inf, cur)
    return jnp.concatenate(vals, axis=0), jnp.concatenate(idxs, axis=0)


def _mod_kernel(c_ref, w_ref, b_ref, o_ref):
    s = _silu(c_ref[...]).astype(BF16)
    o_ref[...] = _dot(s, w_ref[...].astype(BF16)) + b_ref[...]


def _modulation(c_all, w_ada, b_ada):
    r, d = c_all.shape
    n = w_ada.shape[1]
    tn = 1536 if n % 1536 == 0 else n
    return pl.pallas_call(
        _mod_kernel,
        grid=(n // tn,),
        in_specs=[pl.BlockSpec((r, d), lambda j: (0, 0)),
                  pl.BlockSpec((d, tn), lambda j: (0, j)),
                  pl.BlockSpec((1, tn), lambda j: (0, j))],
        out_specs=pl.BlockSpec((r, tn), lambda j: (0, j)),
        out_shape=jax.ShapeDtypeStruct((r, n), F32),
        compiler_params=_cparams("arbitrary"),
        name="adaln_mod",
    )(c_all, w_ada, b_ada.reshape(1, n))


def _in_kernel(x_ref, nw_ref, sc_ref, sh_ref, w_ref, wdt_ref, o_ref, odt_ref, h_scr):
    @pl.when(pl.program_id(1) == 0)
    def _():
        h = _rms(x_ref[...], nw_ref[...]) * (1.0 + sc_ref[...]) + sh_ref[...]
        h_scr[...] = h.astype(BF16)
        odt_ref[...] = _dot(h_scr[...], wdt_ref[...])

    o_ref[...] = _dot(h_scr[...], w_ref[...])


def _in_proj(x, norm_w, scale, shift, w_main, w_dt, tm, tn):
    t, d = x.shape
    n = w_main.shape[1]
    mr = scale.shape[0]
    mod_spec = (pl.BlockSpec((1, d), lambda i, j: (0, 0)) if mr == 1
                else pl.BlockSpec((tm, d), lambda i, j: (i, 0)))
    return pl.pallas_call(
        _in_kernel,
        grid=(t // tm, n // tn),
        in_specs=[pl.BlockSpec((tm, d), lambda i, j: (i, 0)),
                  pl.BlockSpec((1, d), lambda i, j: (0, 0)),
                  mod_spec, mod_spec,
                  pl.BlockSpec((d, tn), lambda i, j: (0, j)),
                  pl.BlockSpec((d, LANES), lambda i, j: (0, 0))],
        out_specs=[pl.BlockSpec((tm, tn), lambda i, j: (i, j)),
                   pl.BlockSpec((tm, LANES), lambda i, j: (i, 0))],
        out_shape=[jax.ShapeDtypeStruct((t, n), F32),
                   jax.ShapeDtypeStruct((t, LANES), F32)],
        scratch_shapes=[pltpu.VMEM((tm, d), BF16)],
        compiler_params=_cparams("arbitrary", "arbitrary"),
        name="in_proj",
    )(x, norm_w.reshape(1, d), scale, shift, w_main, w_dt)


def _moba_prompt_kernel(q_ref, k_ref, v_ref, o_ref, kb_scr, vt_scr, kmean_scr, sel_scr, sc_scr, pr_scr, *, nb):
    g = pl.program_id(1)
    blk = MOBA_BLOCK
    hd = HEAD_DIM

    @pl.when(g == 0)
    def _():
        row = lax.broadcasted_iota(jnp.int32, (2 * hd, blk), 0)
        ones_row = jnp.where(row == hd, 1.0, 0.0)
        for n in range(nb):
            kn = k_ref[pl.ds(n * blk, blk), :]
            kmean_scr[pl.ds(n, 1), :] = jnp.sum(kn, axis=0, keepdims=True) * (1.0 / blk)
            kb_scr[pl.ds(n * blk, blk), :] = kn.astype(BF16)
            vt = v_ref[pl.ds(n * blk, blk), :].T
            vt_scr[0, :, pl.ds(n * blk, blk)] = jnp.where(row < hd, vt, ones_row).astype(BF16)
            vt_sw = jnp.concatenate([vt[hd:], vt[:hd]], axis=0)
            vt_scr[1, :, pl.ds(n * blk, blk)] = jnp.where(row < hd, vt_sw, ones_row).astype(BF16)

    lane_head = lax.broadcasted_iota(jnp.int32, (blk, LANES), 1) // hd
    blk_iota = lax.broadcasted_iota(jnp.int32, (nb, blk), 0)
    key_i = lax.broadcasted_iota(jnp.int32, (blk, blk), 0)
    qry_i = lax.broadcasted_iota(jnp.int32, (blk, blk), 1)
    scale = hd ** -0.5
    nq = MOBA_QBLOCKS
    chains = [(c, hh) for c in range(nq) for hh in range(2)]
    i_last = nq * g + nq - 1

    qs = []
    for ch, (c, hh) in enumerate(chains):
        i = nq * g + c
        qm = jnp.where(lane_head == hh, q_ref[pl.ds(c * blk, blk), :], 0.0)
        s_blk = _dot3(kmean_scr[...], qm, _dot_nt)
        cur = jnp.where(blk_iota < i, s_blk, -jnp.inf)
        sel = jnp.zeros((nb, blk), F32)
        for s in range(min(MOBA_TOPK, nb)):
            m = jnp.max(cur, axis=0, keepdims=True)
            idx = jnp.min(jnp.where(cur == m, blk_iota, nb), axis=0, keepdims=True)
            hit = blk_iota == idx
            sel = jnp.where(hit, jnp.where(i > s, 1.0, sel), sel)
            cur = jnp.where(hit, -jnp.inf, cur)
        sel_scr[ch] = sel
        qs.append((qm * (scale * LOG2E)).astype(BF16))

    span = MOBA_UNROLL * blk

    def scores(off, width, which):
        kb = kb_scr[pl.ds(off, width), :]
        return [_dot_nt(kb, qs[ch]) for ch in which]

    def softmax_step(s_t, masks, m_run):
        s_t = jnp.concatenate([jnp.where(mk, s_t[u * blk:(u + 1) * blk], NEG) for u, mk in enumerate(masks)], axis=0)
        m_new = jnp.maximum(m_run, jnp.max(s_t, axis=0, keepdims=True))
        return m_new, jnp.exp2(m_run - m_new), jnp.exp2(s_t - m_new).astype(BF16)

    def accumulate(hh, off, width, alpha, p, acc):
        return alpha * acc + _dot(vt_scr[hh, :, pl.ds(off, width)], p)

    nsteps = (i_last + MOBA_UNROLL - 1) // MOBA_UNROLL
    max_step = nb // MOBA_UNROLL - 1
    every = list(range(len(chains)))
    for ch, s0 in enumerate(scores(0, span, every)):
        sc_scr[0, ch] = s0
        pr_scr[1, ch] = jnp.zeros((span, blk), BF16)

    def body(k, carry):
        m, acc, alpha_prev = carry
        for cur in range(2):
            nxt = 1 - cur
            t = 2 * k + cur
            live = t < nsteps
            s_next = scores(pl.multiple_of(jnp.minimum(t + 1, max_step) * span, span), span, every)
            off_prev = pl.multiple_of(jnp.maximum(t - 1, 0) * span, span)
            tm = jnp.minimum(t, max_step)
            m_out, acc_out, alpha_out = [], [], []
            for ch, (c, hh) in enumerate(chains):
                sc_scr[nxt, ch] = s_next[ch]
                masks = [(sel_scr[ch, pl.ds(tm * MOBA_UNROLL + u, 1), :] > 0.5) & live for u in range(MOBA_UNROLL)]
                m_new, alpha, p = softmax_step(sc_scr[cur, ch], masks, m[ch])
                acc_out.append(accumulate(hh, off_prev, span, alpha_prev[ch], pr_scr[nxt, ch], acc[ch]))
                pr_scr[cur, ch] = p
                m_out.append(m_new)
                alpha_out.append(alpha)
            m, acc, alpha_prev = m_out, acc_out, alpha_out
        return m, acc, alpha_prev

    zero_row = jnp.zeros((1, blk), F32)
    nch = len(chains)
    init = ([zero_row + NEG] * nch, [jnp.zeros((2 * hd, blk), F32)] * nch, [zero_row + 1.0] * nch)
    ntrips = (nsteps + 1) // 2
    m, acc, alpha_prev = lax.fori_loop(0, ntrips, body, init)
    off_last = pl.multiple_of(jnp.clip(2 * ntrips - 1, 0, max_step) * span, span)
    for c in range(nq):
        off_own = pl.multiple_of((nq * g + c) * blk, blk)
        mine = [ch for ch, (cc, _) in enumerate(chains) if cc == c]
        s_own = scores(off_own, blk, mine)
        outs = []
        for ch, s_o in zip(mine, s_own):
            hh = chains[ch][1]
            a = accumulate(hh, off_last, span, alpha_prev[ch], pr_scr[1, ch], acc[ch])
            _, alpha, p = softmax_step(s_o, [key_i <= qry_i], m[ch])
            a = accumulate(hh, off_own, blk, alpha, p, a)
            outs.append(a[:hd] / a[hd:hd + 1])
        o_ref[pl.ds(c * blk, blk), :] = jnp.concatenate(outs, axis=0).T


def _moba_prompt(proj, t, attn_w):
    nb = t // MOBA_BLOCK
    assert nb % MOBA_UNROLL == 0 and nb % MOBA_QBLOCKS == 0
    npair = attn_w // LANES
    kcol = attn_w // LANES
    qrows = MOBA_QBLOCKS * MOBA_BLOCK
    nch = 2 * MOBA_QBLOCKS
    return pl.pallas_call(
        functools.partial(_moba_prompt_kernel, nb=nb),
        grid=(npair, nb // MOBA_QBLOCKS),
        in_specs=[pl.BlockSpec((qrows, LANES), lambda p, i: (i, p)),
                  pl.BlockSpec((t, LANES), lambda p, i: (0, kcol + p)),
                  pl.BlockSpec((t, LANES), lambda p, i: (0, 2 * kcol + p))],
        out_specs=pl.BlockSpec((qrows, LANES), lambda p, i: (i, p)),
        out_shape=jax.ShapeDtypeStruct((t, attn_w), F32),
        scratch_shapes=[pltpu.VMEM((t, LANES), BF16),
                        pltpu.VMEM((2, LANES, t), BF16),
                        pltpu.VMEM((nb, LANES), F32),
                        pltpu.VMEM((nch, nb, MOBA_BLOCK), F32),
                        pltpu.VMEM((2, nch, MOBA_UNROLL * MOBA_BLOCK, MOBA_BLOCK), F32),
                        pltpu.VMEM((2, nch, MOBA_UNROLL * MOBA_BLOCK, MOBA_BLOCK), BF16)],
        compiler_params=_cparams("arbitrary", "arbitrary"),
        name="moba_prompt",
    )(proj, proj, proj)


def _moba_sample_kernel(pt_ref, q_ref, kn_ref, vn_ref, *rest, nb, nh, lq, pb):
    npg = 2 * pb
    k_refs, v_refs = rest[:npg], rest[npg:2 * npg]
    o_ref, q2_scr, s_scr, mean_scr, max_scr, sel_scr, own_scr, m_scr, l_scr, acc_scr = rest[2 * npg:]
    ph = pl.program_id(1)
    n = pl.program_id(2)
    last = pl.num_programs(2) - 1
    r = nh * lq
    w = nh * HEAD_DIM
    blk = MOBA_BLOCK
    scale = HEAD_DIM ** -0.5
    row_head = lax.broadcasted_iota(jnp.int32, (r, w), 0) // lq
    col_head = lax.broadcasted_iota(jnp.int32, (r, w), 1) // HEAD_DIM
    lane = lax.broadcasted_iota(jnp.int32, (r, LANES), 1)
    own_ok = (lane < lq) & (lane <= lax.broadcasted_iota(jnp.int32, (r, LANES), 0) % lq)

    def block_t(refs, u):
        return jnp.concatenate([refs[2 * u][0], refs[2 * u + 1][0]], axis=1).astype(BF16)

    @pl.when((ph == 0) & (n == 0))
    def _():
        q = q_ref[0]
        qbd = jnp.where(row_head == col_head, jnp.concatenate([q] * nh, axis=0), 0.0)
        hi, lo = _split2(qbd)
        q2_scr[...] = jnp.concatenate([hi, lo], axis=0)
        mean_scr[...] = jnp.zeros(mean_scr.shape, F32)
        max_scr[...] = jnp.zeros(max_scr.shape, F32)

    @pl.when(ph == 0)
    def _():
        means, maxes = mean_scr[...], max_scr[...]
        for u in range(pb):
            s2 = _dot(q2_scr[...], block_t(k_refs, u))
            s = s2[:r] + s2[r:]
            nu = n * pb + u
            s_scr[:, pl.ds(pl.multiple_of(nu * blk, blk), blk)] = s
            means = jnp.where(lane == nu, jnp.sum(s, axis=1, keepdims=True) * (1.0 / blk), means)
            maxes = jnp.where(lane == nu, jnp.max(s, axis=1, keepdims=True), maxes)
        mean_scr[...] = means
        max_scr[...] = maxes

    @pl.when((ph == 0) & (n == last))
    def _():
        cur = jnp.where(lane < nb, mean_scr[...], -jnp.inf)
        sel = jnp.zeros((r, LANES), F32)
        for _ in range(MOBA_TOPK):
            mx = jnp.max(cur, axis=1, keepdims=True)
            idx = jnp.min(jnp.where(cur == mx, lane, LANES), axis=1, keepdims=True)
            hit = lane == idx
            sel = jnp.where(hit, 1.0, sel)
            cur = jnp.where(hit, -jnp.inf, cur)
        sel_scr[...] = sel
        kn = jnp.concatenate([kn_ref[0], jnp.zeros((LANES - lq, w), F32)], axis=0).astype(BF16)
        o2 = _dot_nt(q2_scr[...], kn)
        own = jnp.where(own_ok, o2[:r] + o2[r:], NEG)
        own_scr[...] = own
        m_sel = jnp.max(jnp.where(sel > 0.5, max_scr[...], NEG), axis=1, keepdims=True)
        m_scr[...] = jnp.maximum(m_sel, jnp.max(own, axis=1, keepdims=True))
        l_scr[...] = jnp.zeros(l_scr.shape, F32)
        acc_scr[...] = jnp.zeros(acc_scr.shape, F32)

    @pl.when(ph == 1)
    def _():
        l_add = jnp.zeros((r, 1), F32)
        acc_add = jnp.zeros((r, w), F32)
        for u in range(pb):
            nu = n * pb + u
            picked = jnp.sum(jnp.where(lane == nu, sel_scr[...], 0.0), axis=1, keepdims=True) > 0.5
            s = s_scr[:, pl.ds(pl.multiple_of(nu * blk, blk), blk)]
            p = jnp.where(picked, jnp.exp((s - m_scr[...]) * scale), 0.0)
            l_add = l_add + jnp.sum(p, axis=1, keepdims=True)
            acc_add = acc_add + _dot_nt(p.astype(BF16), block_t(v_refs, u))
        l_scr[...] += l_add
        acc_scr[...] += acc_add

    @pl.when((ph == 1) & (n == last))
    def _():
        p = jnp.where(own_ok, jnp.exp((own_scr[...] - m_scr[...]) * scale), 0.0)
        vn = jnp.concatenate([vn_ref[0], jnp.zeros((LANES - lq, w), F32)], axis=0).astype(BF16)
        l_fin = l_scr[...] + jnp.sum(p, axis=1, keepdims=True)
        acc = acc_scr[...] + _dot(p.astype(BF16), vn)
        full = jnp.where(row_head == col_head, acc / l_fin, 0.0)
        out = full[0:lq]
        for h in range(1, nh):
            out = out + full[h * lq:(h + 1) * lq]
        o_ref[0] = out


def _moba_sample(proj3, cache_kt, cache_vt, page_table, attn_w):
    b, lq, _ = proj3.shape
    _, w, page = cache_kt.shape
    n_pages = page_table.shape[1]
    nh = attn_w // HEAD_DIM
    r = nh * lq
    assert 2 * page == MOBA_BLOCK and page == LANES and w == attn_w and lq <= LANES and r % 8 == 0
    nb = n_pages // 2
    assert nb * 2 == n_pages and MOBA_TOPK <= nb <= LANES
    pb = SAMPLE_BLOCKS if nb % SAMPLE_BLOCKS == 0 else 1
    nsteps = nb // pb
    npg = 2 * pb
    kmap = lambda o: (lambda bi, ph, n, pt: (pt[bi, npg * (n * (1 - ph) + (nsteps - 1) * ph) + o], 0, 0))
    vmap = lambda o: (lambda bi, ph, n, pt: (pt[bi, npg * (n * ph) + o], 0, 0))
    newspec = lambda c: pl.BlockSpec((1, lq, attn_w), lambda bi, ph, n, pt: (bi, 0, c))
    pages = ([pl.BlockSpec((1, w, page), kmap(o)) for o in range(npg)]
             + [pl.BlockSpec((1, w, page), vmap(o)) for o in range(npg)])
    return pl.pallas_call(
        functools.partial(_moba_sample_kernel, nb=nb, nh=nh, lq=lq, pb=pb),
        grid_spec=pltpu.PrefetchScalarGridSpec(
            num_scalar_prefetch=1,
            grid=(b, 2, nsteps),
            in_specs=[newspec(0), newspec(1), newspec(2)] + pages,
            out_specs=pl.BlockSpec((1, lq, attn_w), lambda bi, ph, n, pt: (bi, 0, 0)),
            scratch_shapes=[pltpu.VMEM((2 * r, w), BF16),
                            pltpu.VMEM((r, nb * MOBA_BLOCK), F32),
                            pltpu.VMEM((r, LANES), F32),
                            pltpu.VMEM((r, LANES), F32),
                            pltpu.VMEM((r, LANES), F32),
                            pltpu.VMEM((r, LANES), F32),
                            pltpu.VMEM((r, 1), F32),
                            pltpu.VMEM((r, 1), F32),
                            pltpu.VMEM((r, w), F32)]),
        out_shape=jax.ShapeDtypeStruct((b, lq, attn_w), F32),
        compiler_params=_cparams("arbitrary", "arbitrary", "arbitrary"),
        name="moba_sample",
    )(page_table, proj3, proj3, proj3, *([cache_kt] * npg), *([cache_vt] * npg))


def _ssd_kernel(z_ref, xs_ref, bm_ref, cm_ref, dt_ref, conv0_ref, ssm0_ref,
                cw_ref, cb_ref, dtb_ref, alog_ref, dskip_ref, nw_ref, exp_h_ref, exp_r_ref,
                y_ref, ssm_ref, bx_scr, bb_scr, bc_scr, st_scr, *, lv, nheads):
    c = pl.program_id(1)
    q = SSD_CHUNK
    wx = xs_ref.shape[-1]
    ws = bm_ref.shape[-1]
    pad = 8
    npair = wx // LANES
    rep = nheads // SSD_GROUPS

    @pl.when(c == 0)
    def _():
        bx_scr[...] = jnp.zeros(bx_scr.shape, F32)
        bb_scr[...] = jnp.zeros(bb_scr.shape, F32)
        bc_scr[...] = jnp.zeros(bc_scr.shape, F32)
        c0 = conv0_ref[0]
        lo = pad - (SSD_CONV - 1)
        bx_scr[lo:pad, :] = c0[:, :wx]
        bb_scr[lo:pad, :] = c0[:, wx:wx + ws]
        bc_scr[lo:pad, :] = c0[:, wx + ws:]
        st_scr[...] = ssm0_ref[0]

    def conv(src_ref, buf, c_lo, c_hi):
        buf[pad:pad + lv, :] = src_ref[...]
        acc = cb_ref[:, c_lo:c_hi]
        for i in range(SSD_CONV):
            acc = acc + buf[pl.ds(pad - (SSD_CONV - 1) + i, q), :] * cw_ref[i:i + 1, c_lo:c_hi]
        buf[pad - (SSD_CONV - 1):pad, :] = buf[pad + lv - (SSD_CONV - 1):pad + lv, :]
        return _silu(acc)

    xs = conv(xs_ref, bx_scr, 0, wx)
    bm = conv(bm_ref, bb_scr, wx, wx + ws)
    cm = conv(cm_ref, bc_scr, wx + ws, wx + 2 * ws)

    row = lax.broadcasted_iota(jnp.int32, (q, LANES), 0)
    if lv == q:
        dt_raw = dt_ref[...]
    else:
        dt_raw = jnp.concatenate([dt_ref[...], jnp.zeros((q - lv, LANES), F32)], axis=0)
    dt = jnp.where(row < lv, jax.nn.softplus(dt_raw + dtb_ref[...]), 0.0)
    da = dt * (-jnp.exp(alog_ref[...]))

    li = lax.broadcasted_iota(jnp.int32, (q, q), 0)
    si = lax.broadcasted_iota(jnp.int32, (q, q), 1)
    tril = li >= si
    tri_l = jnp.where(tril, 1.0, 0.0).astype(BF16)
    tri_u = jnp.where(li <= si, 1.0, 0.0).astype(BF16)
    cs = _dot_exact_rhs(da, tri_l, lambda a, t: _dot(t, a))
    cs_t = _dot_exact_rhs(da, tri_u, _dot_tn)
    exp_h = exp_h_ref[...]
    dt_full = _dot_exact_rhs(dt, exp_h)
    cs_full = _dot_exact_rhs(cs, exp_h)
    cs_rep = _dot_exact_rhs(cs, exp_r_ref[...])
    cs_last = cs_full[q - 1:q, :]
    xdt = xs * dt_full
    xdec = (xdt * jnp.exp(cs_last - cs_full)).astype(BF16)
    exp_cs = jnp.exp(cs_full)
    xdt_b = xdt.astype(BF16)
    lane_half = lax.broadcasted_iota(jnp.int32, (q, LANES), 1) // SSD_HEAD_DIM
    rowh = lax.broadcasted_iota(jnp.int32, (LANES, 1), 0) // SSD_HEAD_DIM
    last_t = cs_t[:, q - 1:q]

    g_scores = []
    for g in range(SSD_GROUPS):
        cg = cm[:, g * SSD_STATE:(g + 1) * SSD_STATE].astype(BF16)
        bg = bm[:, g * SSD_STATE:(g + 1) * SSD_STATE].astype(BF16)
        g_scores.append((cg, bg, _dot_nt(cg, bg)))

    ys = []
    for p in range(npair):
        h0 = 2 * p
        cg, bg, sc = g_scores[h0 // rep]
        sl = slice(p * LANES, (p + 1) * LANES)
        xp = xdt_b[:, sl]
        yd = jnp.zeros((q, LANES), F32)
        for hh in range(2):
            h = h0 + hh
            diff = cs_rep[:, h * LANES:h * LANES + q] - cs_t[h:h + 1, :]
            lmat = jnp.exp(jnp.where(tril, diff, NEG))
            mh = (sc * lmat).astype(BF16)
            yd = yd + _dot(mh, jnp.where(lane_half == hh, xp, jnp.zeros_like(xp)))
        st = st_scr[sl, :]
        y_off = _dot_nt(cg, st.astype(BF16)) * exp_cs[:, sl]
        ys.append(yd + y_off)
        dec_col = jnp.where(rowh == 0, jnp.exp(last_t[h0:h0 + 1, :]), jnp.exp(last_t[h0 + 1:h0 + 2, :]))
        st_scr[sl, :] = dec_col * st + _dot_tn(xdec[:, sl], bg)

    y = jnp.concatenate(ys, axis=1) + xs * dskip_ref[...]
    y_ref[...] = _rms(y[:lv] * _silu(z_ref[...]), nw_ref[...])
    ssm_ref[0] = st_scr[...]


def _ssd(proj, proj_dt, conv0, ssm0, conv_w, conv_b, dt_bias, a_log, d_skip, norm_w, b, l, ssd_w):
    ws = SSD_GROUPS * SSD_STATE
    nheads = ssd_w // SSD_HEAD_DIM
    assert nheads <= LANES and nheads % (2 * SSD_GROUPS) == 0 and SSD_STATE == LANES
    lv = min(l, SSD_CHUNK)
    nc = l // lv
    assert nc * lv == l
    attn_cols = proj.shape[1] - 2 * ssd_w - 2 * ws
    assert attn_cols % ssd_w == 0 and (attn_cols + 2 * ssd_w) % ws == 0
    zc = attn_cols // ssd_w
    xc = zc + 1
    bc = (attn_cols + 2 * ssd_w) // ws
    hpad = lambda v: jnp.pad(v.reshape(1, -1), ((0, 0), (0, LANES - nheads)))
    hid = jnp.arange(LANES)[:, None]
    exp_h = (hid == (jnp.arange(ssd_w)[None, :] // SSD_HEAD_DIM)).astype(BF16)
    exp_r = (hid == (jnp.arange(nheads * LANES)[None, :] // LANES)).astype(BF16)
    cch = conv_w.shape[1]
    rowblk = lambda bi, c: bi * nc + c
    const = lambda bi, c: (0, 0)
    return pl.pallas_call(
        functools.partial(_ssd_kernel, lv=lv, nheads=nheads),
        grid=(b, nc),
        in_specs=[pl.BlockSpec((lv, ssd_w), lambda bi, c: (rowblk(bi, c), zc)),
                  pl.BlockSpec((lv, ssd_w), lambda bi, c: (rowblk(bi, c), xc)),
                  pl.BlockSpec((lv, ws), lambda bi, c: (rowblk(bi, c), bc)),
                  pl.BlockSpec((lv, ws), lambda bi, c: (rowblk(bi, c), bc + 1)),
                  pl.BlockSpec((lv, LANES), lambda bi, c: (rowblk(bi, c), 0)),
                  pl.BlockSpec((1, SSD_CONV - 1, cch), lambda bi, c: (bi, 0, 0)),
                  pl.BlockSpec((1, ssd_w, SSD_STATE), lambda bi, c: (bi, 0, 0)),
                  pl.BlockSpec((SSD_CONV, cch), const),
                  pl.BlockSpec((1, cch), const),
                  pl.BlockSpec((1, LANES), const),
                  pl.BlockSpec((1, LANES), const),
                  pl.BlockSpec((1, ssd_w), const),
                  pl.BlockSpec((1, ssd_w), const),
                  pl.BlockSpec((LANES, ssd_w), const),
                  pl.BlockSpec((LANES, nheads * LANES), const)],
        out_specs=[pl.BlockSpec((lv, ssd_w), lambda bi, c: (rowblk(bi, c), 0)),
                   pl.BlockSpec((1, ssd_w, SSD_STATE), lambda bi, c: (bi, 0, 0))],
        out_shape=[jax.ShapeDtypeStruct((b * l, ssd_w), F32),
                   jax.ShapeDtypeStruct((b, ssd_w, SSD_STATE), F32)],
        scratch_shapes=[pltpu.VMEM((SSD_CHUNK + 8, ssd_w), F32),
                        pltpu.VMEM((SSD_CHUNK + 8, ws), F32),
                        pltpu.VMEM((SSD_CHUNK + 8, ws), F32),
                        pltpu.VMEM((ssd_w, SSD_STATE), F32)],
        compiler_params=_cparams("arbitrary", "arbitrary"),
        name="ssd_mixer",
    )(proj, proj, proj, proj, proj_dt, conv0, ssm0, conv_w, conv_b.reshape(1, cch),
      hpad(dt_bias), hpad(a_log), jnp.repeat(d_skip, SSD_HEAD_DIM).reshape(1, ssd_w),
      norm_w.reshape(1, ssd_w), exp_h, exp_r)


def _out_kernel(attn_ref, ssd_ref, x_ref, wa_ref, ws_ref, npost_ref, gate_ref, npre_ref, sc_ref, sh_ref,
                x1_ref, h2_ref):
    mix = _dot(attn_ref[...].astype(BF16), wa_ref[...]) + _dot(ssd_ref[...].astype(BF16), ws_ref[...])
    x1 = x_ref[...] + gate_ref[...] * _rms(mix, npost_ref[...])
    x1_ref[...] = x1
    h2_ref[...] = (_rms(x1, npre_ref[...]) * (1.0 + sc_ref[...]) + sh_ref[...]).astype(BF16)


def _out_proj(attn, ssd, x, w_attn, w_ssd, n_post, gate, n_pre, scale, shift, tm):
    t, d = x.shape
    aw, sw = attn.shape[1], ssd.shape[1]
    mr = gate.shape[0]
    mod_spec = (pl.BlockSpec((1, d), lambda i: (0, 0)) if mr == 1 else pl.BlockSpec((tm, d), lambda i: (i, 0)))
    vec = pl.BlockSpec((1, d), lambda i: (0, 0))
    return pl.pallas_call(
        _out_kernel,
        grid=(t // tm,),
        in_specs=[pl.BlockSpec((tm, aw), lambda i: (i, 0)),
                  pl.BlockSpec((tm, sw), lambda i: (i, 0)),
                  pl.BlockSpec((tm, d), lambda i: (i, 0)),
                  pl.BlockSpec((aw, d), lambda i: (0, 0)),
                  pl.BlockSpec((sw, d), lambda i: (0, 0)),
                  vec, mod_spec, vec, mod_spec, mod_spec],
        out_specs=[pl.BlockSpec((tm, d), lambda i: (i, 0)),
                   pl.BlockSpec((tm, d), lambda i: (i, 0))],
        out_shape=[jax.ShapeDtypeStruct((t, d), F32), jax.ShapeDtypeStruct((t, d), BF16)],
        compiler_params=_cparams("arbitrary"),
        name="out_proj",
    )(attn, ssd, x, w_attn, w_ssd, n_post.reshape(1, d), gate, n_pre.reshape(1, d), scale, shift)


def _route_kernel(h_ref, wq_ref, sk_ref, a_ref, b_ref, g_ref):
    tt = h_ref.shape[0]
    kd = sk_ref.shape[2]
    q = _dot(h_ref[...], wq_ref[...])
    key_iota = lax.broadcasted_iota(jnp.int32, (PEER_KEYS, tt), 0)
    k8 = PEER_TOPK // 2
    row8 = lax.broadcasted_iota(jnp.int32, (k8, tt), 0)
    row16 = lax.broadcasted_iota(jnp.int32, (PEER_TOPK, tt), 0)
    flat = [row16 * PEER_TOPK] + [row8 * PEER_TOPK + kb for kb in range(1, k8)] + [row8 + k8]
    cand_flat = jnp.concatenate(flat, axis=0)
    a_rows, b_rows, g_rows = [], [], []
    for h in range(PEER_HEADS):
        tops = []
        for s in range(2):
            qhs = q[:, (2 * h + s) * kd:(2 * h + s + 1) * kd]
            sc_t = _dot3(sk_ref[s], qhs, _dot_nt)
            tops.append(_topk_rows(sc_t, PEER_TOPK, key_iota))
        (s1, i1), (s2, i2) = tops
        pieces = [s1 + s2[0:1, :]]
        for kb in range(1, k8):
            pieces.append(jnp.where(row8 < PEER_TOPK // (kb + 1), s1[0:k8, :] + s2[kb:kb + 1, :], -jnp.inf))
        pieces.append(s1[0:1, :] + s2[k8:, :])
        top, pos = _topk_rows(jnp.concatenate(pieces, axis=0), PEER_TOPK, cand_flat)
        pa, pb = pos >> 4, pos & (PEER_TOPK - 1)
        ai = jnp.zeros_like(pos)
        bi = jnp.zeros_like(pos)
        for kk in range(PEER_TOPK):
            ai = jnp.where(pa == kk, i1[kk:kk + 1, :], ai)
            bi = jnp.where(pb == kk, i2[kk:kk + 1, :], bi)
        e = jnp.exp(top - jnp.max(top, axis=0, keepdims=True))
        g_rows.append(e / jnp.sum(e, axis=0, keepdims=True))
        a_rows.append(ai)
        b_rows.append(bi)
    a_ref[...] = jnp.concatenate(a_rows, axis=0).astype(F32).T.astype(jnp.int32)
    b_ref[...] = jnp.concatenate(b_rows, axis=0).astype(F32).T.astype(jnp.int32)
    g_ref[...] = jnp.concatenate(g_rows, axis=0).T


def _peer_route(h2, wq, sub_keys, tt):
    t, d = h2.shape
    nq = wq.shape[1]
    ns = PEER_HEADS * PEER_TOPK
    assert ns == LANES and sub_keys.shape[1] == PEER_KEYS == LANES and PEER_TOPK == 16
    tok = pl.BlockSpec((tt, ns), lambda i: (i, 0))
    return pl.pallas_call(
        _route_kernel,
        grid=(t // tt,),
        in_specs=[pl.BlockSpec((tt, d), lambda i: (i, 0)),
                  pl.BlockSpec((d, nq), lambda i: (0, 0)),
                  pl.BlockSpec(sub_keys.shape, lambda i: (0, 0, 0))],
        out_specs=[tok, tok, tok],
        out_shape=[jax.ShapeDtypeStruct((t, ns), jnp.int32), jax.ShapeDtypeStruct((t, ns), jnp.int32),
                   jax.ShapeDtypeStruct((t, ns), F32)],
        compiler_params=_cparams("arbitrary"),
        name="peer_route",
    )(h2, wq, sub_keys)


def _peer_kernel(h_ref, a_ref, b_ref, g_ref, ulo_ref, uhi_ref, vlo_ref, vhi_ref, o_ref, w_scr, *, na):
    j = pl.program_id(1)
    tt = h_ref.shape[0]
    nk = PEER_KEYS
    half = nk // 2

    @pl.when(j == 0)
    def _():
        sub = lax.broadcasted_iota(jnp.int32, (nk, LANES), 0)

        def build(tb, carry):
            for r in range(PEER_BUILD_UNROLL):
                t = tb * PEER_BUILD_UNROLL + r
                arow = a_ref[pl.ds(t, 1), :]
                brow = b_ref[pl.ds(t, 1), :]
                grow = g_ref[pl.ds(t, 1), :]
                oa = jnp.where(sub == arow, 1.0, 0.0).astype(BF16)
                zb = jnp.where(sub == brow, grow, 0.0).astype(BF16)
                w_scr[pl.ds(pl.multiple_of(t * W_PITCH, 8), nk), :] = _dot_nt(oa, zb)
            return carry

        lax.fori_loop(0, tt // PEER_BUILD_UNROLL, build, 0)

    x = h_ref[...]

    def weights(a0):
        return jnp.concatenate([w_scr[pl.ds(a0 + a, tt, stride=W_PITCH), :] for a in range(na)], axis=1)

    p_lo = (weights(j * na) * jax.nn.gelu(_dot_nt(x, ulo_ref[...]))).astype(BF16)
    p_hi = (weights(j * na + half) * jax.nn.gelu(_dot_nt(x, uhi_ref[...]))).astype(BF16)
    contrib = _dot(p_lo, vlo_ref[...]) + _dot(p_hi, vhi_ref[...])

    @pl.when(j == 0)
    def _():
        o_ref[...] = contrib

    @pl.when(j != 0)
    def _():
        o_ref[...] += contrib


def _peer_experts(h2, a_idx, b_idx, g, u, v, tt, na):
    t, d = h2.shape
    ne = u.shape[0]
    eb = na * PEER_KEYS
    nsteps = ne // (2 * eb)
    assert ne == PEER_KEYS * PEER_KEYS and nsteps * 2 * eb == ne and t % tt == 0 and tt % PEER_BUILD_UNROLL == 0
    ns = a_idx.shape[1]
    tok = pl.BlockSpec((tt, ns), lambda i, j: (i, 0))
    lo = pl.BlockSpec((eb, d), lambda i, j: (j, 0))
    hi = pl.BlockSpec((eb, d), lambda i, j: (j + nsteps, 0))
    return pl.pallas_call(
        functools.partial(_peer_kernel, na=na),
        grid=(t // tt, nsteps),
        in_specs=[pl.BlockSpec((tt, d), lambda i, j: (i, 0)), tok, tok, tok, lo, hi, lo, hi],
        out_specs=pl.BlockSpec((tt, d), lambda i, j: (i, 0)),
        out_shape=jax.ShapeDtypeStruct((t, d), F32),
        scratch_shapes=[pltpu.VMEM((tt * W_PITCH, LANES), F32)],
        compiler_params=_cparams("arbitrary", "arbitrary"),
        name="peer_experts",
    )(h2, a_idx, b_idx, g, u, u, v, v)


def _final_kernel(x_ref, f_ref, nw_ref, gate_ref, o_ref):
    o_ref[...] = x_ref[...] + gate_ref[...] * _rms(f_ref[...], nw_ref[...])


def _final(x1, ffn, row0, n_post, gate, tm):
    t, d = x1.shape
    assert row0 % tm == 0
    blk0 = row0 // tm
    mr = gate.shape[0]
    mod_spec = (pl.BlockSpec((1, d), lambda i: (0, 0)) if mr == 1 else pl.BlockSpec((tm, d), lambda i: (i, 0)))
    return pl.pallas_call(
        _final_kernel,
        grid=(t // tm,),
        in_specs=[pl.BlockSpec((tm, d), lambda i: (i, 0)), pl.BlockSpec((tm, d), lambda i: (i + blk0, 0)),
                  pl.BlockSpec((1, d), lambda i: (0, 0)), mod_spec],
        out_specs=pl.BlockSpec((tm, d), lambda i: (i, 0)),
        out_shape=jax.ShapeDtypeStruct((t, d), F32),
        compiler_params=_cparams("arbitrary"),
        name="final_residual",
    )(x1, ffn, n_post.reshape(1, d), gate)


def _tile(n, pref):
    t = min(n, pref)
    while n % t:
        t //= 2
    return t


def kernel(x_prompt, x_sample, c_prompt, c_sample, cache_k, cache_v, page_table, state_conv, state_ssm,
           w_ada, b_ada, norm_mix_pre, norm_mix_post, norm_ffn_pre, norm_ffn_post, w_in, conv_w, conv_b,
           dt_bias, a_log, d_skip, ssd_norm_w, w_out, peer_w_query, peer_sub_keys, peer_u, peer_v):
    bp, lp, d = x_prompt.shape
    bs, ls, _ = x_sample.shape
    assert bp == 1 and lp % MOBA_BLOCK == 0 and ls <= SSD_CHUNK
    attn_w = d // 2
    ssd_w = d - attn_w
    nh_a = attn_w // HEAD_DIM
    nh_s = ssd_w // SSD_HEAD_DIM
    cch = conv_w.shape[1]
    n_main = 3 * attn_w + ssd_w + cch
    n_pool, page, _, _ = cache_k.shape
    tp, ts = bp * lp, bs * ls

    rows = bp + bs
    rpad = -rows % 8
    c_all = jnp.pad(jnp.concatenate([c_prompt, c_sample], axis=0), ((0, rpad), (0, 0)))
    mod = _modulation(c_all, w_ada, b_ada)
    mod_p = [mod[0:bp, k * d:(k + 1) * d] for k in range(6)]
    mod_s = [jnp.repeat(mod[bp:rows, k * d:(k + 1) * d], ls, axis=0) for k in range(6)]

    w_main = w_in[:, :n_main].astype(BF16)
    w_dt = jnp.pad(w_in[:, n_main:], ((0, 0), (0, LANES - nh_s))).astype(BF16)
    w_attn = w_out[:attn_w].astype(BF16)
    w_ssd = w_out[attn_w:].astype(BF16)
    wq = peer_w_query.astype(BF16)
    u_b = peer_u.astype(BF16)
    v_b = peer_v.astype(BF16)

    xp = x_prompt.reshape(tp, d)
    xs = x_sample.reshape(ts, d)
    tn = _tile(n_main, 512)
    proj_p, dt_p = _in_proj(xp, norm_mix_pre, mod_p[1], mod_p[0], w_main, w_dt, _tile(tp, 1024), tn)
    proj_s, dt_s = _in_proj(xs, norm_mix_pre, mod_s[1], mod_s[0], w_main, w_dt, _tile(ts, 256), tn)

    attn_p = _moba_prompt(proj_p, lp, attn_w)
    ck_t = jnp.transpose(cache_k, (0, 2, 3, 1)).reshape(n_pool, attn_w, page)
    cv_t = jnp.transpose(cache_v, (0, 2, 3, 1)).reshape(n_pool, attn_w, page)
    attn_s = _moba_sample(proj_s.reshape(bs, ls, n_main), ck_t, cv_t, page_table, attn_w).reshape(ts, attn_w)

    conv0_p = jnp.zeros((bp, SSD_CONV - 1, cch), F32)
    ssm0_p = jnp.zeros((bp, ssd_w, SSD_STATE), F32)
    ssd_args = (conv_w, conv_b, dt_bias, a_log, d_skip, ssd_norm_w)
    ssd_p, ssm_p = _ssd(proj_p, dt_p, conv0_p, ssm0_p, *ssd_args, bp, lp, ssd_w)
    ssd_s, ssm_s = _ssd(proj_s, dt_s, state_conv, state_ssm.reshape(bs, ssd_w, SSD_STATE), *ssd_args,
                        bs, ls, ssd_w)

    x1_p, h2_p = _out_proj(attn_p, ssd_p, xp, w_attn, w_ssd, norm_mix_post, mod_p[2], norm_ffn_pre,
                           mod_p[4], mod_p[3], _tile(tp, 512))
    x1_s, h2_s = _out_proj(attn_s, ssd_s, xs, w_attn, w_ssd, norm_mix_post, mod_s[2], norm_ffn_pre,
                           mod_s[4], mod_s[3], _tile(ts, 256))

    h2 = jnp.concatenate([h2_p, h2_s], axis=0)
    a_idx, b_idx, g = _peer_route(h2, wq, peer_sub_keys, _tile(tp + ts, 256))
    tt_e = PEER_TILES[0] if (tp + ts) % PEER_TILES[0] == 0 else _tile(tp + ts, 256)
    ffn = _peer_experts(h2, a_idx, b_idx, g, u_b, v_b, tt_e, PEER_TILES[1])

    y_p = _final(x1_p, ffn, 0, norm_ffn_post, mod_p[5], _tile(tp, 512))
    y_s = _final(x1_s, ffn, tp, norm_ffn_post, mod_s[5], _tile(ts, 256))

    kcol, vcol, xcol = attn_w, 2 * attn_w, 3 * attn_w + ssd_w
    k_p = proj_p[:, kcol:kcol + attn_w].reshape(bp, lp, nh_a, HEAD_DIM)
    v_p = proj_p[:, vcol:vcol + attn_w].reshape(bp, lp, nh_a, HEAD_DIM)
    conv_p = proj_p.reshape(bp, lp, n_main)[:, lp - (SSD_CONV - 1):, xcol:xcol + cch]
    k_s = proj_s[:, kcol:kcol + attn_w].reshape(bs, ls, nh_a, HEAD_DIM)
    v_s = proj_s[:, vcol:vcol + attn_w].reshape(bs, ls, nh_a, HEAD_DIM)
    tail = min(ls, SSD_CONV - 1)
    xbc_s = proj_s.reshape(bs, ls, n_main)[:, ls - tail:, xcol:xcol + cch]
    conv_s = jnp.concatenate([state_conv[:, ls:], xbc_s], axis=1)
    return (y_p.reshape(bp, lp, d), y_s.reshape(bs, ls, d), k_p, v_p, conv_p,
            ssm_p.reshape(bp, nh_s, SSD_HEAD_DIM, SSD_STATE), k_s, v_s, conv_s,
            ssm_s.reshape(bs, nh_s, SSD_HEAD_DIM, SSD_STATE))
```

```python
import functools

import jax
import jax.numpy as jnp
from jax import lax
from jax.experimental import pallas as pl
from jax.experimental.pallas import tpu as pltpu

F32 = jnp.float32
BF16 = jnp.bfloat16

HEAD_DIM = 64
MOBA_BLOCK = 256
MOBA_TOPK = 3
SSD_HEAD_DIM = 64
SSD_GROUPS = 2
SSD_STATE = 128
SSD_CONV = 4
SSD_CHUNK = 128
PEER_HEADS = 8
PEER_KEYS = 128
PEER_TOPK = 16
RMS_EPS = 1e-6

LANES = 128
NEG = -1e30
LOG2E = 1.4426950408889634
VMEM_LIMIT = 56 * 1024 * 1024
MOBA_UNROLL = 2
MOBA_QBLOCKS = 2
W_PITCH = 136
PEER_BUILD_UNROLL = 32
PEER_TILES = (384, 4)
SAMPLE_BLOCKS = 8


def _cparams(*sem):
    return pltpu.CompilerParams(dimension_semantics=sem, vmem_limit_bytes=VMEM_LIMIT)


def _dot(a, b):
    return jnp.dot(a, b, preferred_element_type=F32)


def _dot_nt(a, b):
    return lax.dot_general(a, b, (((1,), (1,)), ((), ())), preferred_element_type=F32)


def _dot_tn(a, b):
    return lax.dot_general(a, b, (((0,), (0,)), ((), ())), preferred_element_type=F32)


def _split2(x):
    hi = x.astype(BF16)
    lo = (x - hi.astype(F32)).astype(BF16)
    return hi, lo


def _split3(x):
    hi = x.astype(BF16)
    r = x - hi.astype(F32)
    mid = r.astype(BF16)
    lo = (r - mid.astype(F32)).astype(BF16)
    return hi, mid, lo


def _dot3(a, b, dot=_dot):
    ah, al = _split2(a)
    bh, bl = _split2(b)
    return dot(ah, bh) + (dot(ah, bl) + dot(al, bh))


def _dot_exact_rhs(a, b_bf16, dot=_dot):
    h, m, l = _split3(a)
    return dot(h, b_bf16) + (dot(m, b_bf16) + dot(l, b_bf16))


def _silu(x):
    return x * jax.nn.sigmoid(x)


def _rms(x, w):
    return x * lax.rsqrt(jnp.mean(x * x, axis=-1, keepdims=True) + RMS_EPS) * w


def _topk_rows(cur, k, iota):
    vals, idxs = [], []
    for _ in range(k):
        m = jnp.max(cur, axis=0, keepdims=True)
        idx = jnp.min(jnp.where(cur == m, iota, jnp.iinfo(jnp.int32).max), axis=0, keepdims=True)
        vals.append(m)
        idxs.append(idx)
        cur = jnp.where(iota == idx, -jnp.inf, cur)
    return jnp.concatenate(vals, axis=0), jnp.concatenate(idxs, axis=0)


def _mod_kernel(c_ref, w_ref, b_ref, o_ref):
    s = _silu(c_ref[...]).astype(BF16)
    o_ref[...] = _dot(s, w_ref[...].astype(BF16)) + b_ref[...]


def _modulation(c_all, w_ada, b_ada):
    r, d = c_all.shape
    n = w_ada.shape[1]
    tn = 1536 if n % 1536 == 0 else n
    return pl.pallas_call(
        _mod_kernel,
        grid=(n // tn,),
        in_specs=[pl.BlockSpec((r, d), lambda j: (0, 0)),
                  pl.BlockSpec((d, tn), lambda j: (0, j)),
                  pl.BlockSpec((1, tn), lambda j: (0, j))],
        out_specs=pl.BlockSpec((r, tn), lambda j: (0, j)),
        out_shape=jax.ShapeDtypeStruct((r, n), F32),
        compiler_params=_cparams("arbitrary"),
        name="adaln_mod",
    )(c_all, w_ada, b_ada.reshape(1, n))


def _in_kernel(x_ref, nw_ref, sc_ref, sh_ref, w_ref, wdt_ref, o_ref, odt_ref, h_scr):
    @pl.when(pl.program_id(1) == 0)
    def _():
        h = _rms(x_ref[...], nw_ref[...]) * (1.0 + sc_ref[...]) + sh_ref[...]
        h_scr[...] = h.astype(BF16)
        odt_ref[...] = _dot(h_scr[...], wdt_ref[...])

    o_ref[...] = _dot(h_scr[...], w_ref[...])


def _in_proj(x, norm_w, scale, shift, w_main, w_dt, tm, tn):
    t, d = x.shape
    n = w_main.shape[1]
    mr = scale.shape[0]
    mod_spec = (pl.BlockSpec((1, d), lambda i, j: (0, 0)) if mr == 1
                else pl.BlockSpec((tm, d), lambda i, j: (i, 0)))
    return pl.pallas_call(
        _in_kernel,
        grid=(t // tm, n // tn),
        in_specs=[pl.BlockSpec((tm, d), lambda i, j: (i, 0)),
                  pl.BlockSpec((1, d), lambda i, j: (0, 0)),
                  mod_spec, mod_spec,
                  pl.BlockSpec((d, tn), lambda i, j: (0, j)),
                  pl.BlockSpec((d, LANES), lambda i, j: (0, 0))],
        out_specs=[pl.BlockSpec((tm, tn), lambda i, j: (i, j)),
                   pl.BlockSpec((tm, LANES), lambda i, j: (i, 0))],
        out_shape=[jax.ShapeDtypeStruct((t, n), F32),
                   jax.ShapeDtypeStruct((t, LANES), F32)],
        scratch_shapes=[pltpu.VMEM((tm, d), BF16)],
        compiler_params=_cparams("arbitrary", "arbitrary"),
        name="in_proj",
    )(x, norm_w.reshape(1, d), scale, shift, w_main, w_dt)


def _moba_prompt_kernel(q_ref, k_ref, v_ref, o_ref, kb_scr, vt_scr, kmean_scr, sel_scr, sc_scr, pr_scr, *, nb):
    g = pl.program_id(1)
    blk = MOBA_BLOCK
    hd = HEAD_DIM

    @pl.when(g == 0)
    def _():
        row = lax.broadcasted_iota(jnp.int32, (2 * hd, blk), 0)
        ones_row = jnp.where(row == hd, 1.0, 0.0)
        for n in range(nb):
            kn = k_ref[pl.ds(n * blk, blk), :]
            kmean_scr[pl.ds(n, 1), :] = jnp.sum(kn, axis=0, keepdims=True) * (1.0 / blk)
            kb_scr[pl.ds(n * blk, blk), :] = kn.astype(BF16)
            vt = v_ref[pl.ds(n * blk, blk), :].T
            vt_scr[0, :, pl.ds(n * blk, blk)] = jnp.where(row < hd, vt, ones_row).astype(BF16)
            vt_sw = jnp.concatenate([vt[hd:], vt[:hd]], axis=0)
            vt_scr[1, :, pl.ds(n * blk, blk)] = jnp.where(row < hd, vt_sw, ones_row).astype(BF16)

    lane_head = lax.broadcasted_iota(jnp.int32, (blk, LANES), 1) // hd
    blk_iota = lax.broadcasted_iota(jnp.int32, (nb, blk), 0)
    key_i = lax.broadcasted_iota(jnp.int32, (blk, blk), 0)
    qry_i = lax.broadcasted_iota(jnp.int32, (blk, blk), 1)
    scale = hd ** -0.5
    nq = MOBA_QBLOCKS
    chains = [(c, hh) for c in range(nq) for hh in range(2)]
    i_last = nq * g + nq - 1

    qs = []
    for ch, (c, hh) in enumerate(chains):
        i = nq * g + c
        qm = jnp.where(lane_head == hh, q_ref[pl.ds(c * blk, blk), :], 0.0)
        s_blk = _dot3(kmean_scr[...], qm, _dot_nt)
        cur = jnp.where(blk_iota < i, s_blk, -jnp.inf)
        sel = jnp.zeros((nb, blk), F32)
        for s in range(min(MOBA_TOPK, nb)):
            m = jnp.max(cur, axis=0, keepdims=True)
            idx = jnp.min(jnp.where(cur == m, blk_iota, nb), axis=0, keepdims=True)
            hit = blk_iota == idx
            sel = jnp.where(hit, jnp.where(i > s, 1.0, sel), sel)
            cur = jnp.where(hit, -jnp.inf, cur)
        sel_scr[ch] = sel
        qs.append((qm * (scale * LOG2E)).astype(BF16))

    span = MOBA_UNROLL * blk

    def scores(off, width, which):
        kb = kb_scr[pl.ds(off, width), :]
        return [_dot_nt(kb, qs[ch]) for ch in which]

    def softmax_step(s_t, masks, m_run):
        s_t = jnp.concatenate([jnp.where(mk, s_t[u * blk:(u + 1) * blk], NEG) for u, mk in enumerate(masks)], axis=0)
        m_new = jnp.maximum(m_run, jnp.max(s_t, axis=0, keepdims=True))
        return m_new, jnp.exp2(m_run - m_new), jnp.exp2(s_t - m_new).astype(BF16)

    def accumulate(hh, off, width, alpha, p, acc):
        return alpha * acc + _dot(vt_scr[hh, :, pl.ds(off, width)], p)

    nsteps = (i_last + MOBA_UNROLL - 1) // MOBA_UNROLL
    max_step = nb // MOBA_UNROLL - 1
    every = list(range(len(chains)))
    for ch, s0 in enumerate(scores(0, span, every)):
        sc_scr[0, ch] = s0
        pr_scr[1, ch] = jnp.zeros((span, blk), BF16)

    def body(k, carry):
        m, acc, alpha_prev = carry
        for cur in range(2):
            nxt = 1 - cur
            t = 2 * k + cur
            live = t < nsteps
            s_next = scores(pl.multiple_of(jnp.minimum(t + 1, max_step) * span, span), span, every)
            off_prev = pl.multiple_of(jnp.maximum(t - 1, 0) * span, span)
            tm = jnp.minimum(t, max_step)
            m_out, acc_out, alpha_out = [], [], []
            for ch, (c, hh) in enumerate(chains):
                sc_scr[nxt, ch] = s_next[ch]
                masks = [(sel_scr[ch, pl.ds(tm * MOBA_UNROLL + u, 1), :] > 0.5) & live for u in range(MOBA_UNROLL)]
                m_new, alpha, p = softmax_step(sc_scr[cur, ch], masks, m[ch])
                acc_out.append(accumulate(hh, off_prev, span, alpha_prev[ch], pr_scr[nxt, ch], acc[ch]))
                pr_scr[cur, ch] = p
                m_out.append(m_new)
                alpha_out.append(alpha)
            m, acc, alpha_prev = m_out, acc_out, alpha_out
        return m, acc, alpha_prev

    zero_row = jnp.zeros((1, blk), F32)
    nch = len(chains)
    init = ([zero_row + NEG] * nch, [jnp.zeros((2 * hd, blk), F32)] * nch, [zero_row + 1.0] * nch)
    ntrips = (nsteps + 1) // 2
    m, acc, alpha_prev = lax.fori_loop(0, ntrips, body, init)
    off_last = pl.multiple_of(jnp.clip(2 * ntrips - 1, 0, max_step) * span, span)
    for c in range(nq):
        off_own = pl.multiple_of((nq * g + c) * blk, blk)
        mine = [ch for ch, (cc, _) in enumerate(chains) if cc == c]
        s_own = scores(off_own, blk, mine)
        outs = []
        for ch, s_o in zip(mine, s_own):
            hh = chains[ch][1]
            a = accumulate(hh, off_last, span, alpha_prev[ch], pr_scr[1, ch], acc[ch])
            _, alpha, p = softmax_step(s_o, [key_i <= qry_i], m[ch])
            a = accumulate(hh, off_own, blk, alpha, p, a)
            outs.append(a[:hd] / a[hd:hd + 1])
        o_ref[pl.ds(c * blk, blk), :] = jnp.concatenate(outs, axis=0).T


def _moba_prompt(proj, t, attn_w):
    nb = t // MOBA_BLOCK
    assert nb % MOBA_UNROLL == 0 and nb % MOBA_QBLOCKS == 0
    npair = attn_w // LANES
    kcol = attn_w // LANES
    qrows = MOBA_QBLOCKS * MOBA_BLOCK
    nch = 2 * MOBA_QBLOCKS
    return pl.pallas_call(
        functools.partial(_moba_prompt_kernel, nb=nb),
        grid=(npair, nb // MOBA_QBLOCKS),
        in_specs=[pl.BlockSpec((qrows, LANES), lambda p, i: (i, p)),
                  pl.BlockSpec((t, LANES), lambda p, i: (0, kcol + p)),
                  pl.BlockSpec((t, LANES), lambda p, i: (0, 2 * kcol + p))],
        out_specs=pl.BlockSpec((qrows, LANES), lambda p, i: (i, p)),
        out_shape=jax.ShapeDtypeStruct((t, attn_w), F32),
        scratch_shapes=[pltpu.VMEM((t, LANES), BF16),
                        pltpu.VMEM((2, LANES, t), BF16),
                        pltpu.VMEM((nb, LANES), F32),
                        pltpu.VMEM((nch, nb, MOBA_BLOCK), F32),
                        pltpu.VMEM((2, nch, MOBA_UNROLL * MOBA_BLOCK, MOBA_BLOCK), F32),
                        pltpu.VMEM((2, nch, MOBA_UNROLL * MOBA_BLOCK, MOBA_BLOCK), BF16)],
        compiler_params=_cparams("arbitrary", "arbitrary"),
        name="moba_prompt",
    )(proj, proj, proj)


def _moba_sample_kernel(pt_ref, q_ref, kn_ref, vn_ref, *rest, nb, nh, lq, pb):
    npg = 2 * pb
    k_refs, v_refs = rest[:npg], rest[npg:2 * npg]
    o_ref, q2_scr, s_scr, mean_scr, max_scr, sel_scr, own_scr, m_scr, l_scr, acc_scr = rest[2 * npg:]
    ph = pl.program_id(1)
    n = pl.program_id(2)
    last = pl.num_programs(2) - 1
    r = nh * lq
    w = nh * HEAD_DIM
    blk = MOBA_BLOCK
    scale = HEAD_DIM ** -0.5
    row_head = lax.broadcasted_iota(jnp.int32, (r, w), 0) // lq
    col_head = lax.broadcasted_iota(jnp.int32, (r, w), 1) // HEAD_DIM
    lane = lax.broadcasted_iota(jnp.int32, (r, LANES), 1)
    own_ok = (lane < lq) & (lane <= lax.broadcasted_iota(jnp.int32, (r, LANES), 0) % lq)

    def block_t(refs, u):
        return jnp.concatenate([refs[2 * u][0], refs[2 * u + 1][0]], axis=1).astype(BF16)

    @pl.when((ph == 0) & (n == 0))
    def _():
        q = q_ref[0]
        qbd = jnp.where(row_head == col_head, jnp.concatenate([q] * nh, axis=0), 0.0)
        hi, lo = _split2(qbd)
        q2_scr[...] = jnp.concatenate([hi, lo], axis=0)
        mean_scr[...] = jnp.zeros(mean_scr.shape, F32)
        max_scr[...] = jnp.zeros(max_scr.shape, F32)

    @pl.when(ph == 0)
    def _():
        means, maxes = mean_scr[...], max_scr[...]
        for u in range(pb):
            s2 = _dot(q2_scr[...], block_t(k_refs, u))
            s = s2[:r] + s2[r:]
            nu = n * pb + u
            s_scr[:, pl.ds(pl.multiple_of(nu * blk, blk), blk)] = s
            means = jnp.where(lane == nu, jnp.sum(s, axis=1, keepdims=True) * (1.0 / blk), means)
            maxes = jnp.where(lane == nu, jnp.max(s, axis=1, keepdims=True), maxes)
        mean_scr[...] = means
        max_scr[...] = maxes

    @pl.when((ph == 0) & (n == last))
    def _():
        cur = jnp.where(lane < nb, mean_scr[...], -jnp.inf)
        sel = jnp.zeros((r, LANES), F32)
        for _ in range(MOBA_TOPK):
            mx = jnp.max(cur, axis=1, keepdims=True)
            idx = jnp.min(jnp.where(cur == mx, lane, LANES), axis=1, keepdims=True)
            hit = lane == idx
            sel = jnp.where(hit, 1.0, sel)
            cur = jnp.where(hit, -jnp.inf, cur)
        sel_scr[...] = sel
        kn = jnp.concatenate([kn_ref[0], jnp.zeros((LANES - lq, w), F32)], axis=0).astype(BF16)
        o2 = _dot_nt(q2_scr[...], kn)
        own = jnp.where(own_ok, o2[:r] + o2[r:], NEG)
        own_scr[...] = own
        m_sel = jnp.max(jnp.where(sel > 0.5, max_scr[...], NEG), axis=1, keepdims=True)
        m_scr[...] = jnp.maximum(m_sel, jnp.max(own, axis=1, keepdims=True))
        l_scr[...] = jnp.zeros(l_scr.shape, F32)
        acc_scr[...] = jnp.zeros(acc_scr.shape, F32)

    @pl.when(ph == 1)
    def _():
        l_add = jnp.zeros((r, 1), F32)
        acc_add = jnp.zeros((r, w), F32)
        for u in range(pb):
            nu = n * pb + u
            picked = jnp.sum(jnp.where(lane == nu, sel_scr[...], 0.0), axis=1, keepdims=True) > 0.5
            s = s_scr[:, pl.ds(pl.multiple_of(nu * blk, blk), blk)]
            p = jnp.where(picked, jnp.exp((s - m_scr[...]) * scale), 0.0)
            l_add = l_add + jnp.sum(p, axis=1, keepdims=True)
            acc_add = acc_add + _dot_nt(p.astype(BF16), block_t(v_refs, u))
        l_scr[...] += l_add
        acc_scr[...] += acc_add

    @pl.when((ph == 1) & (n == last))
    def _():
        p = jnp.where(own_ok, jnp.exp((own_scr[...] - m_scr[...]) * scale), 0.0)
        vn = jnp.concatenate([vn_ref[0], jnp.zeros((LANES - lq, w), F32)], axis=0).astype(BF16)
        l_fin = l_scr[...] + jnp.sum(p, axis=1, keepdims=True)
        acc = acc_scr[...] + _dot(p.astype(BF16), vn)
        full = jnp.where(row_head == col_head, acc / l_fin, 0.0)
        out = full[0:lq]
        for h in range(1, nh):
            out = out + full[h * lq:(h + 1) * lq]
        o_ref[0] = out


def _moba_sample(proj3, cache_kt, cache_vt, page_table, attn_w):
    b, lq, _ = proj3.shape
    _, w, page = cache_kt.shape
    n_pages = page_table.shape[1]
    nh = attn_w // HEAD_DIM
    r = nh * lq
    assert 2 * page == MOBA_BLOCK and page == LANES and w == attn_w and lq <= LANES and r % 8 == 0
    nb = n_pages // 2
    assert nb * 2 == n_pages and MOBA_TOPK <= nb <= LANES
    pb = SAMPLE_BLOCKS if nb % SAMPLE_BLOCKS == 0 else 1
    nsteps = nb // pb
    npg = 2 * pb
    kmap = lambda o: (lambda bi, ph, n, pt: (pt[bi, npg * (n * (1 - ph) + (nsteps - 1) * ph) + o], 0, 0))
    vmap = lambda o: (lambda bi, ph, n, pt: (pt[bi, npg * (n * ph) + o], 0, 0))
    newspec = lambda c: pl.BlockSpec((1, lq, attn_w), lambda bi, ph, n, pt: (bi, 0, c))
    pages = ([pl.BlockSpec((1, w, page), kmap(o)) for o in range(npg)]
             + [pl.BlockSpec((1, w, page), vmap(o)) for o in range(npg)])
    return pl.pallas_call(
        functools.partial(_moba_sample_kernel, nb=nb, nh=nh, lq=lq, pb=pb),
        grid_spec=pltpu.PrefetchScalarGridSpec(
            num_scalar_prefetch=1,
            grid=(b, 2, nsteps),
            in_specs=[newspec(0), newspec(1), newspec(2)] + pages,
            out_specs=pl.BlockSpec((1, lq, attn_w), lambda bi, ph, n, pt: (bi, 0, 0)),
            scratch_shapes=[pltpu.VMEM((2 * r, w), BF16),
                            pltpu.VMEM((r, nb * MOBA_BLOCK), F32),
                            pltpu.VMEM((r, LANES), F32),
                            pltpu.VMEM((r, LANES), F32),
                            pltpu.VMEM((r, LANES), F32),
                            pltpu.VMEM((r, LANES), F32),
                            pltpu.VMEM((r, 1), F32),
                            pltpu.VMEM((r, 1), F32),
                            pltpu.VMEM((r, w), F32)]),
        out_shape=jax.ShapeDtypeStruct((b, lq, attn_w), F32),
        compiler_params=_cparams("arbitrary", "arbitrary", "arbitrary"),
        name="moba_sample",
    )(page_table, proj3, proj3, proj3, *([cache_kt] * npg), *([cache_vt] * npg))


def _ssd_kernel(z_ref, xs_ref, bm_ref, cm_ref, dt_ref, conv0_ref, ssm0_ref,
                cw_ref, cb_ref, dtb_ref, alog_ref, dskip_ref, nw_ref, exp_h_ref, exp_r_ref,
                y_ref, ssm_ref, bx_scr, bb_scr, bc_scr, st_scr, *, lv, nheads):
    c = pl.program_id(1)
    q = SSD_CHUNK
    wx = xs_ref.shape[-1]
    ws = bm_ref.shape[-1]
    pad = 8
    npair = wx // LANES
    rep = nheads // SSD_GROUPS

    @pl.when(c == 0)
    def _():
        bx_scr[...] = jnp.zeros(bx_scr.shape, F32)
        bb_scr[...] = jnp.zeros(bb_scr.shape, F32)
        bc_scr[...] = jnp.zeros(bc_scr.shape, F32)
        c0 = conv0_ref[0]
        lo = pad - (SSD_CONV - 1)
        bx_scr[lo:pad, :] = c0[:, :wx]
        bb_scr[lo:pad, :] = c0[:, wx:wx + ws]
        bc_scr[lo:pad, :] = c0[:, wx + ws:]
        st_scr[...] = ssm0_ref[0]

    def conv(src_ref, buf, c_lo, c_hi):
        buf[pad:pad + lv, :] = src_ref[...]
        acc = cb_ref[:, c_lo:c_hi]
        for i in range(SSD_CONV):
            acc = acc + buf[pl.ds(pad - (SSD_CONV - 1) + i, q), :] * cw_ref[i:i + 1, c_lo:c_hi]
        buf[pad - (SSD_CONV - 1):pad, :] = buf[pad + lv - (SSD_CONV - 1):pad + lv, :]
        return _silu(acc)

    xs = conv(xs_ref, bx_scr, 0, wx)
    bm = conv(bm_ref, bb_scr, wx, wx + ws)
    cm = conv(cm_ref, bc_scr, wx + ws, wx + 2 * ws)

    row = lax.broadcasted_iota(jnp.int32, (q, LANES), 0)
    if lv == q:
        dt_raw = dt_ref[...]
    else:
        dt_raw = jnp.concatenate([dt_ref[...], jnp.zeros((q - lv, LANES), F32)], axis=0)
    dt = jnp.where(row < lv, jax.nn.softplus(dt_raw + dtb_ref[...]), 0.0)
    da = dt * (-jnp.exp(alog_ref[...]))

    li = lax.broadcasted_iota(jnp.int32, (q, q), 0)
    si = lax.broadcasted_iota(jnp.int32, (q, q), 1)
    tril = li >= si
    tri_l = jnp.where(tril, 1.0, 0.0).astype(BF16)
    tri_u = jnp.where(li <= si, 1.0, 0.0).astype(BF16)
    cs = _dot_exact_rhs(da, tri_l, lambda a, t: _dot(t, a))
    cs_t = _dot_exact_rhs(da, tri_u, _dot_tn)
    exp_h = exp_h_ref[...]
    dt_full = _dot_exact_rhs(dt, exp_h)
    cs_full = _dot_exact_rhs(cs, exp_h)
    cs_rep = _dot_exact_rhs(cs, exp_r_ref[...])
    cs_last = cs_full[q - 1:q, :]
    xdt = xs * dt_full
    xdec = (xdt * jnp.exp(cs_last - cs_full)).astype(BF16)
    exp_cs = jnp.exp(cs_full)
    xdt_b = xdt.astype(BF16)
    lane_half = lax.broadcasted_iota(jnp.int32, (q, LANES), 1) // SSD_HEAD_DIM
    rowh = lax.broadcasted_iota(jnp.int32, (LANES, 1), 0) // SSD_HEAD_DIM
    last_t = cs_t[:, q - 1:q]

    g_scores = []
    for g in range(SSD_GROUPS):
        cg = cm[:, g * SSD_STATE:(g + 1) * SSD_STATE].astype(BF16)
        bg = bm[:, g * SSD_STATE:(g + 1) * SSD_STATE].astype(BF16)
        g_scores.append((cg, bg, _dot_nt(cg, bg)))

    ys = []
    for p in range(npair):
        h0 = 2 * p
        cg, bg, sc = g_scores[h0 // rep]
        sl = slice(p * LANES, (p + 1) * LANES)
        xp = xdt_b[:, sl]
        yd = jnp.zeros((q, LANES), F32)
        for hh in range(2):
            h = h0 + hh
            diff = cs_rep[:, h * LANES:h * LANES + q] - cs_t[h:h + 1, :]
            lmat = jnp.exp(jnp.where(tril, diff, NEG))
            mh = (sc * lmat).astype(BF16)
            yd = yd + _dot(mh, jnp.where(lane_half == hh, xp, jnp.zeros_like(xp)))
        st = st_scr[sl, :]
        y_off = _dot_nt(cg, st.astype(BF16)) * exp_cs[:, sl]
        ys.append(yd + y_off)
        dec_col = jnp.where(rowh == 0, jnp.exp(last_t[h0:h0 + 1, :]), jnp.exp(last_t[h0 + 1:h0 + 2, :]))
        st_scr[sl, :] = dec_col * st + _dot_tn(xdec[:, sl], bg)

    y = jnp.concatenate(ys, axis=1) + xs * dskip_ref[...]
    y_ref[...] = _rms(y[:lv] * _silu(z_ref[...]), nw_ref[...])
    ssm_ref[0] = st_scr[...]


def _ssd(proj, proj_dt, conv0, ssm0, conv_w, conv_b, dt_bias, a_log, d_skip, norm_w, b, l, ssd_w):
    ws = SSD_GROUPS * SSD_STATE
    nheads = ssd_w // SSD_HEAD_DIM
    assert nheads <= LANES and nheads % (2 * SSD_GROUPS) == 0 and SSD_STATE == LANES
    lv = min(l, SSD_CHUNK)
    nc = l // lv
    assert nc * lv == l
    attn_cols = proj.shape[1] - 2 * ssd_w - 2 * ws
    assert attn_cols % ssd_w == 0 and (attn_cols + 2 * ssd_w) % ws == 0
    zc = attn_cols // ssd_w
    xc = zc + 1
    bc = (attn_cols + 2 * ssd_w) // ws
    hpad = lambda v: jnp.pad(v.reshape(1, -1), ((0, 0), (0, LANES - nheads)))
    hid = jnp.arange(LANES)[:, None]
    exp_h = (hid == (jnp.arange(ssd_w)[None, :] // SSD_HEAD_DIM)).astype(BF16)
    exp_r = (hid == (jnp.arange(nheads * LANES)[None, :] // LANES)).astype(BF16)
    cch = conv_w.shape[1]
    rowblk = lambda bi, c: bi * nc + c
    const = lambda bi, c: (0, 0)
    return pl.pallas_call(
        functools.partial(_ssd_kernel, lv=lv, nheads=nheads),
        grid=(b, nc),
        in_specs=[pl.BlockSpec((lv, ssd_w), lambda bi, c: (rowblk(bi, c), zc)),
                  pl.BlockSpec((lv, ssd_w), lambda bi, c: (rowblk(bi, c), xc)),
                  pl.BlockSpec((lv, ws), lambda bi, c: (rowblk(bi, c), bc)),
                  pl.BlockSpec((lv, ws), lambda bi, c: (rowblk(bi, c), bc + 1)),
                  pl.BlockSpec((lv, LANES), lambda bi, c: (rowblk(bi, c), 0)),
                  pl.BlockSpec((1, SSD_CONV - 1, cch), lambda bi, c: (bi, 0, 0)),
                  pl.BlockSpec((1, ssd_w, SSD_STATE), lambda bi, c: (bi, 0, 0)),
                  pl.BlockSpec((SSD_CONV, cch), const),
                  pl.BlockSpec((1, cch), const),
                  pl.BlockSpec((1, LANES), const),
                  pl.BlockSpec((1, LANES), const),
                  pl.BlockSpec((1, ssd_w), const),
                  pl.BlockSpec((1, ssd_w), const),
                  pl.BlockSpec((LANES, ssd_w), const),
                  pl.BlockSpec((LANES, nheads * LANES), const)],
        out_specs=[pl.BlockSpec((lv, ssd_w), lambda bi, c: (rowblk(bi, c), 0)),
                   pl.BlockSpec((1, ssd_w, SSD_STATE), lambda bi, c: (bi, 0, 0))],
        out_shape=[jax.ShapeDtypeStruct((b * l, ssd_w), F32),
                   jax.ShapeDtypeStruct((b, ssd_w, SSD_STATE), F32)],
        scratch_shapes=[pltpu.VMEM((SSD_CHUNK + 8, ssd_w), F32),
                        pltpu.VMEM((SSD_CHUNK + 8, ws), F32),
                        pltpu.VMEM((SSD_CHUNK + 8, ws), F32),
                        pltpu.VMEM((ssd_w, SSD_STATE), F32)],
        compiler_params=_cparams("arbitrary", "arbitrary"),
        name="ssd_mixer",
    )(proj, proj, proj, proj, proj_dt, conv0, ssm0, conv_w, conv_b.reshape(1, cch),
      hpad(dt_bias), hpad(a_log), jnp.repeat(d_skip, SSD_HEAD_DIM).reshape(1, ssd_w),
      norm_w.reshape(1, ssd_w), exp_h, exp_r)


def _out_kernel(attn_ref, ssd_ref, x_ref, wa_ref, ws_ref, npost_ref, gate_ref, npre_ref, sc_ref, sh_ref,
                x1_ref, h2_ref):
    mix = _dot(attn_ref[...].astype(BF16), wa_ref[...]) + _dot(ssd_ref[...].astype(BF16), ws_ref[...])
    x1 = x_ref[...] + gate_ref[...] * _rms(mix, npost_ref[...])
    x1_ref[...] = x1
    h2_ref[...] = (_rms(x1, npre_ref[...]) * (1.0 + sc_ref[...]) + sh_ref[...]).astype(BF16)


def _out_proj(attn, ssd, x, w_attn, w_ssd, n_post, gate, n_pre, scale, shift, tm):
    t, d = x.shape
    aw, sw = attn.shape[1], ssd.shape[1]
    mr = gate.shape[0]
    mod_spec = (pl.BlockSpec((1, d), lambda i: (0, 0)) if mr == 1 else pl.BlockSpec((tm, d), lambda i: (i, 0)))
    vec = pl.BlockSpec((1, d), lambda i: (0, 0))
    return pl.pallas_call(
        _out_kernel,
        grid=(t // tm,),
        in_specs=[pl.BlockSpec((tm, aw), lambda i: (i, 0)),
                  pl.BlockSpec((tm, sw), lambda i: (i, 0)),
                  pl.BlockSpec((tm, d), lambda i: (i, 0)),
                  pl.BlockSpec((aw, d), lambda i: (0, 0)),
                  pl.BlockSpec((sw, d), lambda i: (0, 0)),
                  vec, mod_spec, vec, mod_spec, mod_spec],
        out_specs=[pl.BlockSpec((tm, d), lambda i: (i, 0)),
                   pl.BlockSpec((tm, d), lambda i: (i, 0))],
        out_shape=[jax.ShapeDtypeStruct((t, d), F32), jax.ShapeDtypeStruct((t, d), BF16)],
        compiler_params=_cparams("arbitrary"),
        name="out_proj",
    )(attn, ssd, x, w_attn, w_ssd, n_post.reshape(1, d), gate, n_pre.reshape(1, d), scale, shift)


def _route_kernel(h_ref, wq_ref, sk_ref, a_ref, b_ref, g_ref):
    tt = h_ref.shape[0]
    kd = sk_ref.shape[2]
    q = _dot(h_ref[...], wq_ref[...])
    key_iota = lax.broadcasted_iota(jnp.int32, (PEER_KEYS, tt), 0)
    k8 = PEER_TOPK // 2
    row8 = lax.broadcasted_iota(jnp.int32, (k8, tt), 0)
    row16 = lax.broadcasted_iota(jnp.int32, (PEER_TOPK, tt), 0)
    flat = [row16 * PEER_TOPK] + [row8 * PEER_TOPK + kb for kb in range(1, k8)] + [row8 + k8]
    cand_flat = jnp.concatenate(flat, axis=0)
    a_rows, b_rows, g_rows = [], [], []
    for h in range(PEER_HEADS):
        tops = []
        for s in range(2):
            qhs = q[:, (2 * h + s) * kd:(2 * h + s + 1) * kd]
            sc_t = _dot3(sk_ref[s], qhs, _dot_nt)
            tops.append(_topk_rows(sc_t, PEER_TOPK, key_iota))
        (s1, i1), (s2, i2) = tops
        pieces = [s1 + s2[0:1, :]]
        for kb in range(1, k8):
            pieces.append(jnp.where(row8 < PEER_TOPK // (kb + 1), s1[0:k8, :] + s2[kb:kb + 1, :], -jnp.inf))
        pieces.append(s1[0:1, :] + s2[k8:, :])
        top, pos = _topk_rows(jnp.concatenate(pieces, axis=0), PEER_TOPK, cand_flat)
        pa, pb = pos >> 4, pos & (PEER_TOPK - 1)
        ai = jnp.zeros_like(pos)
        bi = jnp.zeros_like(pos)
        for kk in range(PEER_TOPK):
            ai = jnp.where(pa == kk, i1[kk:kk + 1, :], ai)
            bi = jnp.where(pb == kk, i2[kk:kk + 1, :], bi)
        e = jnp.exp(top - jnp.max(top, axis=0, keepdims=True))
        g_rows.append(e / jnp.sum(e, axis=0, keepdims=True))
        a_rows.append(ai)
        b_rows.append(bi)
    a_ref[...] = jnp.concatenate(a_rows, axis=0).astype(F32).T.astype(jnp.int32)
    b_ref[...] = jnp.concatenate(b_rows, axis=0).astype(F32).T.astype(jnp.int32)
    g_ref[...] = jnp.concatenate(g_rows, axis=0).T


def _peer_route(h2, wq, sub_keys, tt):
    t, d = h2.shape
    nq = wq.shape[1]
    ns = PEER_HEADS * PEER_TOPK
    assert ns == LANES and sub_keys.shape[1] == PEER_KEYS == LANES and PEER_TOPK == 16
    tok = pl.BlockSpec((tt, ns), lambda i: (i, 0))
    return pl.pallas_call(
        _route_kernel,
        grid=(t // tt,),
        in_specs=[pl.BlockSpec((tt, d), lambda i: (i, 0)),
                  pl.BlockSpec((d, nq), lambda i: (0, 0)),
                  pl.BlockSpec(sub_keys.shape, lambda i: (0, 0, 0))],
        out_specs=[tok, tok, tok],
        out_shape=[jax.ShapeDtypeStruct((t, ns), jnp.int32), jax.ShapeDtypeStruct((t, ns), jnp.int32),
                   jax.ShapeDtypeStruct((t, ns), F32)],
        compiler_params=_cparams("arbitrary"),
        name="peer_route",
    )(h2, wq, sub_keys)


def _peer_kernel(h_ref, a_ref, b_ref, g_ref, ulo_ref, uhi_ref, vlo_ref, vhi_ref, o_ref, w_scr, *, na):
    j = pl.program_id(1)
    tt = h_ref.shape[0]
    nk = PEER_KEYS
    half = nk // 2

    @pl.when(j == 0)
    def _():
        sub = lax.broadcasted_iota(jnp.int32, (nk, LANES), 0)

        def build(tb, carry):
            for r in range(PEER_BUILD_UNROLL):
                t = tb * PEER_BUILD_UNROLL + r
                arow = a_ref[pl.ds(t, 1), :]
                brow = b_ref[pl.ds(t, 1), :]
                grow = g_ref[pl.ds(t, 1), :]
                oa = jnp.where(sub == arow, 1.0, 0.0).astype(BF16)
                zb = jnp.where(sub == brow, grow, 0.0).astype(BF16)
                w_scr[pl.ds(pl.multiple_of(t * W_PITCH, 8), nk), :] = _dot_nt(oa, zb)
            return carry

        lax.fori_loop(0, tt // PEER_BUILD_UNROLL, build, 0)

    x = h_ref[...]

    def weights(a0):
        return jnp.concatenate([w_scr[pl.ds(a0 + a, tt, stride=W_PITCH), :] for a in range(na)], axis=1)

    p_lo = (weights(j * na) * jax.nn.gelu(_dot_nt(x, ulo_ref[...]))).astype(BF16)
    p_hi = (weights(j * na + half) * jax.nn.gelu(_dot_nt(x, uhi_ref[...]))).astype(BF16)
    contrib = _dot(p_lo, vlo_ref[...]) + _dot(p_hi, vhi_ref[...])

    @pl.when(j == 0)
    def _():
        o_ref[...] = contrib

    @pl.when(j != 0)
    def _():
        o_ref[...] += contrib


def _peer_experts(h2, a_idx, b_idx, g, u, v, tt, na):
    t, d = h2.shape
    ne = u.shape[0]
    eb = na * PEER_KEYS
    nsteps = ne // (2 * eb)
    assert ne == PEER_KEYS * PEER_KEYS and nsteps * 2 * eb == ne and t % tt == 0 and tt % PEER_BUILD_UNROLL == 0
    ns = a_idx.shape[1]
    tok = pl.BlockSpec((tt, ns), lambda i, j: (i, 0))
    lo = pl.BlockSpec((eb, d), lambda i, j: (j, 0))
    hi = pl.BlockSpec((eb, d), lambda i, j: (j + nsteps, 0))
    return pl.pallas_call(
        functools.partial(_peer_kernel, na=na),
        grid=(t // tt, nsteps),
        in_specs=[pl.BlockSpec((tt, d), lambda i, j: (i, 0)), tok, tok, tok, lo, hi, lo, hi],
        out_specs=pl.BlockSpec((tt, d), lambda i, j: (i, 0)),
        out_shape=jax.ShapeDtypeStruct((t, d), F32),
        scratch_shapes=[pltpu.VMEM((tt * W_PITCH, LANES), F32)],
        compiler_params=_cparams("arbitrary", "arbitrary"),
        name="peer_experts",
    )(h2, a_idx, b_idx, g, u, u, v, v)


def _final_kernel(x_ref, f_ref, nw_ref, gate_ref, o_ref):
    o_ref[...] = x_ref[...] + gate_ref[...] * _rms(f_ref[...], nw_ref[...])


def _final(x1, ffn, row0, n_post, gate, tm):
    t, d = x1.shape
    assert row0 % tm == 0
    blk0 = row0 // tm
    mr = gate.shape[0]
    mod_spec = (pl.BlockSpec((1, d), lambda i: (0, 0)) if mr == 1 else pl.BlockSpec((tm, d), lambda i: (i, 0)))
    return pl.pallas_call(
        _final_kernel,
        grid=(t // tm,),
        in_specs=[pl.BlockSpec((tm, d), lambda i: (i, 0)), pl.BlockSpec((tm, d), lambda i: (i + blk0, 0)),
                  pl.BlockSpec((1, d), lambda i: (0, 0)), mod_spec],
        out_specs=pl.BlockSpec((tm, d), lambda i: (i, 0)),
        out_shape=jax.ShapeDtypeStruct((t, d), F32),
        compiler_params=_cparams("arbitrary"),
        name="final_residual",
    )(x1, ffn, n_post.reshape(1, d), gate)


def _tile(n, pref):
    t = min(n, pref)
    while n % t:
        t //= 2
    return t


def kernel(x_prompt, x_sample, c_prompt, c_sample, cache_k, cache_v, page_table, state_conv, state_ssm,
           w_ada, b_ada, norm_mix_pre, norm_mix_post, norm_ffn_pre, norm_ffn_post, w_in, conv_w, conv_b,
           dt_bias, a_log, d_skip, ssd_norm_w, w_out, peer_w_query, peer_sub_keys, peer_u, peer_v):
    bp, lp, d = x_prompt.shape
    bs, ls, _ = x_sample.shape
    assert bp == 1 and lp % MOBA_BLOCK == 0 and ls <= SSD_CHUNK
    attn_w = d // 2
    ssd_w = d - attn_w
    nh_a = attn_w // HEAD_DIM
    nh_s = ssd_w // SSD_HEAD_DIM
    cch = conv_w.shape[1]
    n_main = 3 * attn_w + ssd_w + cch
    n_pool, page, _, _ = cache_k.shape
    tp, ts = bp * lp, bs * ls

    rows = bp + bs
    rpad = -rows % 8
    c_all = jnp.pad(jnp.concatenate([c_prompt, c_sample], axis=0), ((0, rpad), (0, 0)))
    mod = _modulation(c_all, w_ada, b_ada)
    mod_p = [mod[0:bp, k * d:(k + 1) * d] for k in range(6)]
    mod_s = [jnp.repeat(mod[bp:rows, k * d:(k + 1) * d], ls, axis=0) for k in range(6)]

    w_main = w_in[:, :n_main].astype(BF16)
    w_dt = jnp.pad(w_in[:, n_main:], ((0, 0), (0, LANES - nh_s))).astype(BF16)
    w_attn = w_out[:attn_w].astype(BF16)
    w_ssd = w_out[attn_w:].astype(BF16)
    wq = peer_w_query.astype(BF16)
    u_b = peer_u.astype(BF16)
    v_b = peer_v.astype(BF16)

    xp = x_prompt.reshape(tp, d)
    xs = x_sample.reshape(ts, d)
    tn = _tile(n_main, 512)
    proj_p, dt_p = _in_proj(xp, norm_mix_pre, mod_p[1], mod_p[0], w_main, w_dt, _tile(tp, 1024), tn)
    proj_s, dt_s = _in_proj(xs, norm_mix_pre, mod_s[1], mod_s[0], w_main, w_dt, _tile(ts, 256),
                            _tile(n_main, n_main // 2))

    attn_p = _moba_prompt(proj_p, lp, attn_w)
    ck_t = jnp.transpose(cache_k, (0, 2, 3, 1)).reshape(n_pool, attn_w, page)
    cv_t = jnp.transpose(cache_v, (0, 2, 3, 1)).reshape(n_pool, attn_w, page)
    attn_s = _moba_sample(proj_s.reshape(bs, ls, n_main), ck_t, cv_t, page_table, attn_w).reshape(ts, attn_w)

    conv0_p = jnp.zeros((bp, SSD_CONV - 1, cch), F32)
    ssm0_p = jnp.zeros((bp, ssd_w, SSD_STATE), F32)
    ssd_args = (conv_w, conv_b, dt_bias, a_log, d_skip, ssd_norm_w)
    ssd_p, ssm_p = _ssd(proj_p, dt_p, conv0_p, ssm0_p, *ssd_args, bp, lp, ssd_w)
    ssd_s, ssm_s = _ssd(proj_s, dt_s, state_conv, state_ssm.reshape(bs, ssd_w, SSD_STATE), *ssd_args,
                        bs, ls, ssd_w)

    x1_p, h2_p = _out_proj(attn_p, ssd_p, xp, w_attn, w_ssd, norm_mix_post, mod_p[2], norm_ffn_pre,
                           mod_p[4], mod_p[3], _tile(tp, 512))
    x1_s, h2_s = _out_proj(attn_s, ssd_s, xs, w_attn, w_ssd, norm_mix_post, mod_s[2], norm_ffn_pre,
                           mod_s[4], mod_s[3], _tile(ts, 256))

    h2 = jnp.concatenate([h2_p, h2_s], axis=0)
    a_idx, b_idx, g = _peer_route(h2, wq, peer_sub_keys, _tile(tp + ts, 256))
    tt_e = PEER_TILES[0] if (tp + ts) % PEER_TILES[0] == 0 else _tile(tp + ts, 256)
    ffn = _peer_experts(h2, a_idx, b_idx, g, u_b, v_b, tt_e, PEER_TILES[1])

    y_p = _final(x1_p, ffn, 0, norm_ffn_post, mod_p[5], _tile(tp, 512))
    y_s = _final(x1_s, ffn, tp, norm_ffn_post, mod_s[5], _tile(ts, 256))

    kcol, vcol, xcol = attn_w, 2 * attn_w, 3 * attn_w + ssd_w
    k_p = proj_p[:, kcol:kcol + attn_w].reshape(bp, lp, nh_a, HEAD_DIM)
    v_p = proj_p[:, vcol:vcol + attn_w].reshape(bp, lp, nh_a, HEAD_DIM)
    conv_p = proj_p.reshape(bp, lp, n_main)[:, lp - (SSD_CONV - 1):, xcol:xcol + cch]
    k_s = proj_s[:, kcol:kcol + attn_w].reshape(bs, ls, nh_a, HEAD_DIM)
    v_s = proj_s[:, vcol:vcol + attn_w].reshape(bs, ls, nh_a, HEAD_DIM)
    tail = min(ls, SSD_CONV - 1)
    xbc_s = proj_s.reshape(bs, ls, n_main)[:, ls - tail:, xcol:xcol + cch]
    conv_s = jnp.concatenate([state_conv[:, ls:], xbc_s], axis=1)
    return (y_p.reshape(bp, lp, d), y_s.reshape(bs, ls, d), k_p, v_p, conv_p,
            ssm_p.reshape(bp, nh_s, SSD_HEAD_DIM, SSD_STATE), k_s, v_s, conv_s,
            ssm_s.reshape(bs, nh_s, SSD_HEAD_DIM, SSD_STATE))
```

```python
import functools

import jax
import jax.numpy as jnp
from jax import lax
from jax.experimental import pallas as pl
from jax.experimental.pallas import tpu as pltpu

F32 = jnp.float32
BF16 = jnp.bfloat16

HEAD_DIM = 64
MOBA_BLOCK = 256
MOBA_TOPK = 3
SSD_HEAD_DIM = 64
SSD_GROUPS = 2
SSD_STATE = 128
SSD_CONV = 4
SSD_CHUNK = 128
PEER_HEADS = 8
PEER_KEYS = 128
PEER_TOPK = 16
RMS_EPS = 1e-6

LANES = 128
NEG = -1e30
LOG2E = 1.4426950408889634
VMEM_LIMIT = 56 * 1024 * 1024
MOBA_UNROLL = 2
MOBA_QBLOCKS = 4
W_PITCH = 136
PEER_BUILD_UNROLL = 32
PEER_TILES = (384, 4)
SAMPLE_BLOCKS = 8


def _cparams(*sem):
    return pltpu.CompilerParams(dimension_semantics=sem, vmem_limit_bytes=VMEM_LIMIT)


def _dot(a, b):
    return jnp.dot(a, b, preferred_element_type=F32)


def _dot_nt(a, b):
    return lax.dot_general(a, b, (((1,), (1,)), ((), ())), preferred_element_type=F32)


def _dot_tn(a, b):
    return lax.dot_general(a, b, (((0,), (0,)), ((), ())), preferred_element_type=F32)


def _split2(x):
    hi = x.astype(BF16)
    lo = (x - hi.astype(F32)).astype(BF16)
    return hi, lo


def _split3(x):
    hi = x.astype(BF16)
    r = x - hi.astype(F32)
    mid = r.astype(BF16)
    lo = (r - mid.astype(F32)).astype(BF16)
    return hi, mid, lo


def _dot3(a, b, dot=_dot):
    ah, al = _split2(a)
    bh, bl = _split2(b)
    return dot(ah, bh) + (dot(ah, bl) + dot(al, bh))


def _dot_exact_rhs(a, b_bf16, dot=_dot):
    h, m, l = _split3(a)
    return dot(h, b_bf16) + (dot(m, b_bf16) + dot(l, b_bf16))


def _silu(x):
    return x * jax.nn.sigmoid(x)


def _rms(x, w):
    return x * lax.rsqrt(jnp.mean(x * x, axis=-1, keepdims=True) + RMS_EPS) * w


def _topk_rows(cur, k, iota):
    vals, idxs = [], []
    for _ in range(k):
        m = jnp.max(cur, axis=0, keepdims=True)
        idx = jnp.min(jnp.where(cur == m, iota, jnp.iinfo(jnp.int32).max), axis=0, keepdims=True)
        vals.append(m)
        idxs.append(idx)
        cur = jnp.where(iota == idx, -jnp.inf, cur)
    return jnp.concatenate(vals, axis=0), jnp.concatenate(idxs, axis=0)


def _mod_kernel(c_ref, w_ref, b_ref, o_ref):
    s = _silu(c_ref[...]).astype(BF16)
    o_ref[...] = _dot(s, w_ref[...].astype(BF16)) + b_ref[...]


def _modulation(c_all, w_ada, b_ada):
    r, d = c_all.shape
    n = w_ada.shape[1]
    tn = 1536 if n % 1536 == 0 else n
    return pl.pallas_call(
        _mod_kernel,
        grid=(n // tn,),
        in_specs=[pl.BlockSpec((r, d), lambda j: (0, 0)),
                  pl.BlockSpec((d, tn), lambda j: (0, j)),
                  pl.BlockSpec((1, tn), lambda j: (0, j))],
        out_specs=pl.BlockSpec((r, tn), lambda j: (0, j)),
        out_shape=jax.ShapeDtypeStruct((r, n), F32),
        compiler_params=_cparams("arbitrary"),
        name="adaln_mod",
    )(c_all, w_ada, b_ada.reshape(1, n))


def _in_kernel(x_ref, nw_ref, sc_ref, sh_ref, w_ref, wdt_ref, o_ref, odt_ref, h_scr):
    @pl.when(pl.program_id(1) == 0)
    def _():
        h = _rms(x_ref[...], nw_ref[...]) * (1.0 + sc_ref[...]) + sh_ref[...]
        h_scr[...] = h.astype(BF16)
        odt_ref[...] = _dot_nt(h_scr[...], wdt_ref[...])

    o_ref[...] = _dot_nt(h_scr[...], w_ref[...])


def _in_proj(x, norm_w, scale, shift, w_main, w_dt, tm, tn):
    t, d = x.shape
    n = w_main.shape[0]
    mr = scale.shape[0]
    mod_spec = (pl.BlockSpec((1, d), lambda i, j: (0, 0)) if mr == 1
                else pl.BlockSpec((tm, d), lambda i, j: (i, 0)))
    return pl.pallas_call(
        _in_kernel,
        grid=(t // tm, n // tn),
        in_specs=[pl.BlockSpec((tm, d), lambda i, j: (i, 0)),
                  pl.BlockSpec((1, d), lambda i, j: (0, 0)),
                  mod_spec, mod_spec,
                  pl.BlockSpec((tn, d), lambda i, j: (j, 0)),
                  pl.BlockSpec((LANES, d), lambda i, j: (0, 0))],
        out_specs=[pl.BlockSpec((tm, tn), lambda i, j: (i, j)),
                   pl.BlockSpec((tm, LANES), lambda i, j: (i, 0))],
        out_shape=[jax.ShapeDtypeStruct((t, n), F32),
                   jax.ShapeDtypeStruct((t, LANES), F32)],
        scratch_shapes=[pltpu.VMEM((tm, d), BF16)],
        compiler_params=_cparams("arbitrary", "arbitrary"),
        name="in_proj",
    )(x, norm_w.reshape(1, d), scale, shift, w_main, w_dt)


def _moba_prompt_kernel(q_ref, k_ref, v_ref, o_ref, kb_scr, vt_scr, kmean_scr, sel_scr, sc_scr, pr_scr, *, nb):
    g = pl.program_id(1)
    blk = MOBA_BLOCK
    hd = HEAD_DIM

    @pl.when(g == 0)
    def _():
        row = lax.broadcasted_iota(jnp.int32, (2 * hd, blk), 0)
        ones_row = jnp.where(row == hd, 1.0, 0.0)
        for n in range(nb):
            kn = k_ref[pl.ds(n * blk, blk), :]
            kmean_scr[pl.ds(n, 1), :] = jnp.sum(kn, axis=0, keepdims=True) * (1.0 / blk)
            kb_scr[pl.ds(n * blk, blk), :] = kn.astype(BF16)
            vt = v_ref[pl.ds(n * blk, blk), :].T
            vt_scr[0, :, pl.ds(n * blk, blk)] = jnp.where(row < hd, vt, ones_row).astype(BF16)
            vt_sw = jnp.concatenate([vt[hd:], vt[:hd]], axis=0)
            vt_scr[1, :, pl.ds(n * blk, blk)] = jnp.where(row < hd, vt_sw, ones_row).astype(BF16)

    lane_head = lax.broadcasted_iota(jnp.int32, (blk, LANES), 1) // hd
    blk_iota = lax.broadcasted_iota(jnp.int32, (nb, blk), 0)
    key_i = lax.broadcasted_iota(jnp.int32, (blk, blk), 0)
    qry_i = lax.broadcasted_iota(jnp.int32, (blk, blk), 1)
    scale = hd ** -0.5
    nq = MOBA_QBLOCKS
    chains = [(c, hh) for c in range(nq) for hh in range(2)]
    i_last = nq * g + nq - 1

    qs = []
    for ch, (c, hh) in enumerate(chains):
        i = nq * g + c
        qm = jnp.where(lane_head == hh, q_ref[pl.ds(c * blk, blk), :], 0.0)
        s_blk = _dot3(kmean_scr[...], qm, _dot_nt)
        cur = jnp.where(blk_iota < i, s_blk, -jnp.inf)
        sel = jnp.zeros((nb, blk), F32)
        for s in range(min(MOBA_TOPK, nb)):
            m = jnp.max(cur, axis=0, keepdims=True)
            idx = jnp.min(jnp.where(cur == m, blk_iota, nb), axis=0, keepdims=True)
            hit = blk_iota == idx
            sel = jnp.where(hit, jnp.where(i > s, 1.0, sel), sel)
            cur = jnp.where(hit, -jnp.inf, cur)
        sel_scr[ch] = sel
        qs.append((qm * (scale * LOG2E)).astype(BF16))

    span = MOBA_UNROLL * blk

    def scores(off, width, which):
        kb = kb_scr[pl.ds(off, width), :]
        return [_dot_nt(kb, qs[ch]) for ch in which]

    def softmax_step(s_t, masks, m_run):
        s_t = jnp.concatenate([jnp.where(mk, s_t[u * blk:(u + 1) * blk], NEG) for u, mk in enumerate(masks)], axis=0)
        m_new = jnp.maximum(m_run, jnp.max(s_t, axis=0, keepdims=True))
        return m_new, jnp.exp2(m_run - m_new), jnp.exp2(s_t - m_new).astype(BF16)

    def accumulate(hh, off, width, alpha, p, acc):
        return alpha * acc + _dot(vt_scr[hh, :, pl.ds(off, width)], p)

    nsteps = (i_last + MOBA_UNROLL - 1) // MOBA_UNROLL
    max_step = nb // MOBA_UNROLL - 1
    every = list(range(len(chains)))
    for ch, s0 in enumerate(scores(0, span, every)):
        sc_scr[0, ch] = s0
        pr_scr[1, ch] = jnp.zeros((span, blk), BF16)

    def body(k, carry):
        m, acc, alpha_prev = carry
        for cur in range(2):
            nxt = 1 - cur
            t = 2 * k + cur
            live = t < nsteps
            s_next = scores(pl.multiple_of(jnp.minimum(t + 1, max_step) * span, span), span, every)
            off_prev = pl.multiple_of(jnp.maximum(t - 1, 0) * span, span)
            tm = jnp.minimum(t, max_step)
            m_out, acc_out, alpha_out = [], [], []
            for ch, (c, hh) in enumerate(chains):
                sc_scr[nxt, ch] = s_next[ch]
                masks = [(sel_scr[ch, pl.ds(tm * MOBA_UNROLL + u, 1), :] > 0.5) & live for u in range(MOBA_UNROLL)]
                m_new, alpha, p = softmax_step(sc_scr[cur, ch], masks, m[ch])
                acc_out.append(accumulate(hh, off_prev, span, alpha_prev[ch], pr_scr[nxt, ch], acc[ch]))
                pr_scr[cur, ch] = p
                m_out.append(m_new)
                alpha_out.append(alpha)
            m, acc, alpha_prev = m_out, acc_out, alpha_out
        return m, acc, alpha_prev

    zero_row = jnp.zeros((1, blk), F32)
    nch = len(chains)
    init = ([zero_row + NEG] * nch, [jnp.zeros((2 * hd, blk), F32)] * nch, [zero_row + 1.0] * nch)
    ntrips = (nsteps + 1) // 2
    m, acc, alpha_prev = lax.fori_loop(0, ntrips, body, init)
    off_last = pl.multiple_of(jnp.clip(2 * ntrips - 1, 0, max_step) * span, span)
    for c in range(nq):
        off_own = pl.multiple_of((nq * g + c) * blk, blk)
        mine = [ch for ch, (cc, _) in enumerate(chains) if cc == c]
        s_own = scores(off_own, blk, mine)
        outs = []
        for ch, s_o in zip(mine, s_own):
            hh = chains[ch][1]
            a = accumulate(hh, off_last, span, alpha_prev[ch], pr_scr[1, ch], acc[ch])
            _, alpha, p = softmax_step(s_o, [key_i <= qry_i], m[ch])
            a = accumulate(hh, off_own, blk, alpha, p, a)
            outs.append(a[:hd] / a[hd:hd + 1])
        o_ref[pl.ds(c * blk, blk), :] = jnp.concatenate(outs, axis=0).T


def _moba_prompt(proj, t, attn_w):
    nb = t // MOBA_BLOCK
    assert nb % MOBA_UNROLL == 0 and nb % MOBA_QBLOCKS == 0
    npair = attn_w // LANES
    kcol = attn_w // LANES
    qrows = MOBA_QBLOCKS * MOBA_BLOCK
    nch = 2 * MOBA_QBLOCKS
    return pl.pallas_call(
        functools.partial(_moba_prompt_kernel, nb=nb),
        grid=(npair, nb // MOBA_QBLOCKS),
        in_specs=[pl.BlockSpec((qrows, LANES), lambda p, i: (i, p)),
                  pl.BlockSpec((t, LANES), lambda p, i: (0, kcol + p)),
                  pl.BlockSpec((t, LANES), lambda p, i: (0, 2 * kcol + p))],
        out_specs=pl.BlockSpec((qrows, LANES), lambda p, i: (i, p)),
        out_shape=jax.ShapeDtypeStruct((t, attn_w), F32),
        scratch_shapes=[pltpu.VMEM((t, LANES), BF16),
                        pltpu.VMEM((2, LANES, t), BF16),
                        pltpu.VMEM((nb, LANES), F32),
                        pltpu.VMEM((nch, nb, MOBA_BLOCK), F32),
                        pltpu.VMEM((2, nch, MOBA_UNROLL * MOBA_BLOCK, MOBA_BLOCK), F32),
                        pltpu.VMEM((2, nch, MOBA_UNROLL * MOBA_BLOCK, MOBA_BLOCK), BF16)],
        compiler_params=_cparams("arbitrary", "arbitrary"),
        name="moba_prompt",
    )(proj, proj, proj)


def _moba_sample_kernel(pt_ref, q_ref, kn_ref, vn_ref, *rest, nb, nh, lq, pb):
    npg = 2 * pb
    k_refs, v_refs = rest[:npg], rest[npg:2 * npg]
    o_ref, q2_scr, s_scr, mean_scr, max_scr, sel_scr, own_scr, m_scr, l_scr, acc_scr = rest[2 * npg:]
    ph = pl.program_id(1)
    n = pl.program_id(2)
    last = pl.num_programs(2) - 1
    r = nh * lq
    w = nh * HEAD_DIM
    blk = MOBA_BLOCK
    scale = HEAD_DIM ** -0.5
    row_head = lax.broadcasted_iota(jnp.int32, (r, w), 0) // lq
    col_head = lax.broadcasted_iota(jnp.int32, (r, w), 1) // HEAD_DIM
    lane = lax.broadcasted_iota(jnp.int32, (r, LANES), 1)
    own_ok = (lane < lq) & (lane <= lax.broadcasted_iota(jnp.int32, (r, LANES), 0) % lq)

    def block_t(refs, u):
        return jnp.concatenate([refs[2 * u][0], refs[2 * u + 1][0]], axis=1).astype(BF16)

    @pl.when((ph == 0) & (n == 0))
    def _():
        q = q_ref[0]
        qbd = jnp.where(row_head == col_head, jnp.concatenate([q] * nh, axis=0), 0.0)
        hi, lo = _split2(qbd)
        q2_scr[...] = jnp.concatenate([hi, lo], axis=0)
        mean_scr[...] = jnp.zeros(mean_scr.shape, F32)
        max_scr[...] = jnp.zeros(max_scr.shape, F32)

    @pl.when(ph == 0)
    def _():
        means, maxes = mean_scr[...], max_scr[...]
        for u in range(pb):
            s2 = _dot(q2_scr[...], block_t(k_refs, u))
            s = s2[:r] + s2[r:]
            nu = n * pb + u
            s_scr[:, pl.ds(pl.multiple_of(nu * blk, blk), blk)] = s
            means = jnp.where(lane == nu, jnp.sum(s, axis=1, keepdims=True) * (1.0 / blk), means)
            maxes = jnp.where(lane == nu, jnp.max(s, axis=1, keepdims=True), maxes)
        mean_scr[...] = means
        max_scr[...] = maxes

    @pl.when((ph == 0) & (n == last))
    def _():
        cur = jnp.where(lane < nb, mean_scr[...], -jnp.inf)
        sel = jnp.zeros((r, LANES), F32)
        for _ in range(MOBA_TOPK):
            mx = jnp.max(cur, axis=1, keepdims=True)
            idx = jnp.min(jnp.where(cur == mx, lane, LANES), axis=1, keepdims=True)
            hit = lane == idx
            sel = jnp.where(hit, 1.0, sel)
            cur = jnp.where(hit, -jnp.inf, cur)
        sel_scr[...] = sel
        kn = jnp.concatenate([kn_ref[0], jnp.zeros((LANES - lq, w), F32)], axis=0).astype(BF16)
        o2 = _dot_nt(q2_scr[...], kn)
        own = jnp.where(own_ok, o2[:r] + o2[r:], NEG)
        own_scr[...] = own
        m_sel = jnp.max(jnp.where(sel > 0.5, max_scr[...], NEG), axis=1, keepdims=True)
        m_scr[...] = jnp.maximum(m_sel, jnp.max(own, axis=1, keepdims=True))
        l_scr[...] = jnp.zeros(l_scr.shape, F32)
        acc_scr[...] = jnp.zeros(acc_scr.shape, F32)

    @pl.when(ph == 1)
    def _():
        l_add = jnp.zeros((r, 1), F32)
        acc_add = jnp.zeros((r, w), F32)
        for u in range(pb):
            nu = n * pb + u
            picked = jnp.sum(jnp.where(lane == nu, sel_scr[...], 0.0), axis=1, keepdims=True) > 0.5
            s = s_scr[:, pl.ds(pl.multiple_of(nu * blk, blk), blk)]
            p = jnp.where(picked, jnp.exp((s - m_scr[...]) * scale), 0.0)
            l_add = l_add + jnp.sum(p, axis=1, keepdims=True)
            acc_add = acc_add + _dot_nt(p.astype(BF16), block_t(v_refs, u))
        l_scr[...] += l_add
        acc_scr[...] += acc_add

    @pl.when((ph == 1) & (n == last))
    def _():
        p = jnp.where(own_ok, jnp.exp((own_scr[...] - m_scr[...]) * scale), 0.0)
        vn = jnp.concatenate([vn_ref[0], jnp.zeros((LANES - lq, w), F32)], axis=0).astype(BF16)
        l_fin = l_scr[...] + jnp.sum(p, axis=1, keepdims=True)
        acc = acc_scr[...] + _dot(p.astype(BF16), vn)
        full = jnp.where(row_head == col_head, acc / l_fin, 0.0)
        out = full[0:lq]
        for h in range(1, nh):
            out = out + full[h * lq:(h + 1) * lq]
        o_ref[0] = out


def _moba_sample(proj3, cache_kt, cache_vt, page_table, attn_w):
    b, lq, _ = proj3.shape
    _, w, page = cache_kt.shape
    n_pages = page_table.shape[1]
    nh = attn_w // HEAD_DIM
    r = nh * lq
    assert 2 * page == MOBA_BLOCK and page == LANES and w == attn_w and lq <= LANES and r % 8 == 0
    nb = n_pages // 2
    assert nb * 2 == n_pages and MOBA_TOPK <= nb <= LANES
    pb = SAMPLE_BLOCKS if nb % SAMPLE_BLOCKS == 0 else 1
    nsteps = nb // pb
    npg = 2 * pb
    kmap = lambda o: (lambda bi, ph, n, pt: (pt[bi, npg * (n * (1 - ph) + (nsteps - 1) * ph) + o], 0, 0))
    vmap = lambda o: (lambda bi, ph, n, pt: (pt[bi, npg * (n * ph) + o], 0, 0))
    newspec = lambda c: pl.BlockSpec((1, lq, attn_w), lambda bi, ph, n, pt: (bi, 0, c))
    pages = ([pl.BlockSpec((1, w, page), kmap(o)) for o in range(npg)]
             + [pl.BlockSpec((1, w, page), vmap(o)) for o in range(npg)])
    return pl.pallas_call(
        functools.partial(_moba_sample_kernel, nb=nb, nh=nh, lq=lq, pb=pb),
        grid_spec=pltpu.PrefetchScalarGridSpec(
            num_scalar_prefetch=1,
            grid=(b, 2, nsteps),
            in_specs=[newspec(0), newspec(1), newspec(2)] + pages,
            out_specs=pl.BlockSpec((1, lq, attn_w), lambda bi, ph, n, pt: (bi, 0, 0)),
            scratch_shapes=[pltpu.VMEM((2 * r, w), BF16),
                            pltpu.VMEM((r, nb * MOBA_BLOCK), F32),
                            pltpu.VMEM((r, LANES), F32),
                            pltpu.VMEM((r, LANES), F32),
                            pltpu.VMEM((r, LANES), F32),
                            pltpu.VMEM((r, LANES), F32),
                            pltpu.VMEM((r, 1), F32),
                            pltpu.VMEM((r, 1), F32),
                            pltpu.VMEM((r, w), F32)]),
        out_shape=jax.ShapeDtypeStruct((b, lq, attn_w), F32),
        compiler_params=_cparams("arbitrary", "arbitrary", "arbitrary"),
        name="moba_sample",
    )(page_table, proj3, proj3, proj3, *([cache_kt] * npg), *([cache_vt] * npg))


def _ssd_kernel(z_ref, xs_ref, bm_ref, cm_ref, dt_ref, conv0_ref, ssm0_ref,
                cw_ref, cb_ref, dtb_ref, alog_ref, dskip_ref, nw_ref, exp_h_ref, exp_r_ref,
                y_ref, ssm_ref, bx_scr, bb_scr, bc_scr, st_scr, *, lv, nheads):
    c = pl.program_id(1)
    q = SSD_CHUNK
    wx = xs_ref.shape[-1]
    ws = bm_ref.shape[-1]
    pad = 8
    npair = wx // LANES
    rep = nheads // SSD_GROUPS

    @pl.when(c == 0)
    def _():
        bx_scr[...] = jnp.zeros(bx_scr.shape, F32)
        bb_scr[...] = jnp.zeros(bb_scr.shape, F32)
        bc_scr[...] = jnp.zeros(bc_scr.shape, F32)
        c0 = conv0_ref[0]
        lo = pad - (SSD_CONV - 1)
        bx_scr[lo:pad, :] = c0[:, :wx]
        bb_scr[lo:pad, :] = c0[:, wx:wx + ws]
        bc_scr[lo:pad, :] = c0[:, wx + ws:]
        st_scr[...] = ssm0_ref[0]

    def conv(src_ref, buf, c_lo, c_hi):
        buf[pad:pad + lv, :] = src_ref[...]
        acc = cb_ref[:, c_lo:c_hi]
        for i in range(SSD_CONV):
            acc = acc + buf[pl.ds(pad - (SSD_CONV - 1) + i, q), :] * cw_ref[i:i + 1, c_lo:c_hi]
        buf[pad - (SSD_CONV - 1):pad, :] = buf[pad + lv - (SSD_CONV - 1):pad + lv, :]
        return _silu(acc)

    xs = conv(xs_ref, bx_scr, 0, wx)
    bm = conv(bm_ref, bb_scr, wx, wx + ws)
    cm = conv(cm_ref, bc_scr, wx + ws, wx + 2 * ws)

    row = lax.broadcasted_iota(jnp.int32, (q, LANES), 0)
    if lv == q:
        dt_raw = dt_ref[...]
    else:
        dt_raw = jnp.concatenate([dt_ref[...], jnp.zeros((q - lv, LANES), F32)], axis=0)
    dt = jnp.where(row < lv, jax.nn.softplus(dt_raw + dtb_ref[...]), 0.0)
    da = dt * (-jnp.exp(alog_ref[...]))

    li = lax.broadcasted_iota(jnp.int32, (q, q), 0)
    si = lax.broadcasted_iota(jnp.int32, (q, q), 1)
    tril = li >= si
    tri_l = jnp.where(tril, 1.0, 0.0).astype(BF16)
    tri_u = jnp.where(li <= si, 1.0, 0.0).astype(BF16)
    cs = _dot_exact_rhs(da, tri_l, lambda a, t: _dot(t, a))
    cs_t = _dot_exact_rhs(da, tri_u, _dot_tn)
    exp_h = exp_h_ref[...]
    dt_full = _dot_exact_rhs(dt, exp_h)
    cs_full = _dot_exact_rhs(cs, exp_h)
    cs_rep = _dot_exact_rhs(cs, exp_r_ref[...])
    cs_last = cs_full[q - 1:q, :]
    xdt = xs * dt_full
    xdec = (xdt * jnp.exp(cs_last - cs_full)).astype(BF16)
    exp_cs = jnp.exp(cs_full)
    xdt_b = xdt.astype(BF16)
    lane_half = lax.broadcasted_iota(jnp.int32, (q, LANES), 1) // SSD_HEAD_DIM
    rowh = lax.broadcasted_iota(jnp.int32, (LANES, 1), 0) // SSD_HEAD_DIM
    last_t = cs_t[:, q - 1:q]

    g_scores = []
    for g in range(SSD_GROUPS):
        cg = cm[:, g * SSD_STATE:(g + 1) * SSD_STATE].astype(BF16)
        bg = bm[:, g * SSD_STATE:(g + 1) * SSD_STATE].astype(BF16)
        g_scores.append((cg, bg, _dot_nt(cg, bg)))

    ys = []
    for p in range(npair):
        h0 = 2 * p
        cg, bg, sc = g_scores[h0 // rep]
        sl = slice(p * LANES, (p + 1) * LANES)
        xp = xdt_b[:, sl]
        yd = jnp.zeros((q, LANES), F32)
        for hh in range(2):
            h = h0 + hh
            diff = cs_rep[:, h * LANES:h * LANES + q] - cs_t[h:h + 1, :]
            lmat = jnp.exp(jnp.where(tril, diff, NEG))
            mh = (sc * lmat).astype(BF16)
            yd = yd + _dot(mh, jnp.where(lane_half == hh, xp, jnp.zeros_like(xp)))
        st = st_scr[sl, :]
        y_off = _dot_nt(cg, st.astype(BF16)) * exp_cs[:, sl]
        ys.append(yd + y_off)
        dec_col = jnp.where(rowh == 0, jnp.exp(last_t[h0:h0 + 1, :]), jnp.exp(last_t[h0 + 1:h0 + 2, :]))
        st_scr[sl, :] = dec_col * st + _dot_tn(xdec[:, sl], bg)

    y = jnp.concatenate(ys, axis=1) + xs * dskip_ref[...]
    y_ref[...] = _rms(y[:lv] * _silu(z_ref[...]), nw_ref[...])
    ssm_ref[0] = st_scr[...]


def _ssd(proj, proj_dt, conv0, ssm0, conv_w, conv_b, dt_bias, a_log, d_skip, norm_w, b, l, ssd_w):
    ws = SSD_GROUPS * SSD_STATE
    nheads = ssd_w // SSD_HEAD_DIM
    assert nheads <= LANES and nheads % (2 * SSD_GROUPS) == 0 and SSD_STATE == LANES
    lv = min(l, SSD_CHUNK)
    nc = l // lv
    assert nc * lv == l
    attn_cols = proj.shape[1] - 2 * ssd_w - 2 * ws
    assert attn_cols % ssd_w == 0 and (attn_cols + 2 * ssd_w) % ws == 0
    zc = attn_cols // ssd_w
    xc = zc + 1
    bc = (attn_cols + 2 * ssd_w) // ws
    hpad = lambda v: jnp.pad(v.reshape(1, -1), ((0, 0), (0, LANES - nheads)))
    hid = jnp.arange(LANES)[:, None]
    exp_h = (hid == (jnp.arange(ssd_w)[None, :] // SSD_HEAD_DIM)).astype(BF16)
    exp_r = (hid == (jnp.arange(nheads * LANES)[None, :] // LANES)).astype(BF16)
    cch = conv_w.shape[1]
    rowblk = lambda bi, c: bi * nc + c
    const = lambda bi, c: (0, 0)
    return pl.pallas_call(
        functools.partial(_ssd_kernel, lv=lv, nheads=nheads),
        grid=(b, nc),
        in_specs=[pl.BlockSpec((lv, ssd_w), lambda bi, c: (rowblk(bi, c), zc)),
                  pl.BlockSpec((lv, ssd_w), lambda bi, c: (rowblk(bi, c), xc)),
                  pl.BlockSpec((lv, ws), lambda bi, c: (rowblk(bi, c), bc)),
                  pl.BlockSpec((lv, ws), lambda bi, c: (rowblk(bi, c), bc + 1)),
                  pl.BlockSpec((lv, LANES), lambda bi, c: (rowblk(bi, c), 0)),
                  pl.BlockSpec((1, SSD_CONV - 1, cch), lambda bi, c: (bi, 0, 0)),
                  pl.BlockSpec((1, ssd_w, SSD_STATE), lambda bi, c: (bi, 0, 0)),
                  pl.BlockSpec((SSD_CONV, cch), const),
                  pl.BlockSpec((1, cch), const),
                  pl.BlockSpec((1, LANES), const),
                  pl.BlockSpec((1, LANES), const),
                  pl.BlockSpec((1, ssd_w), const),
                  pl.BlockSpec((1, ssd_w), const),
                  pl.BlockSpec((LANES, ssd_w), const),
                  pl.BlockSpec((LANES, nheads * LANES), const)],
        out_specs=[pl.BlockSpec((lv, ssd_w), lambda bi, c: (rowblk(bi, c), 0)),
                   pl.BlockSpec((1, ssd_w, SSD_STATE), lambda bi, c: (bi, 0, 0))],
        out_shape=[jax.ShapeDtypeStruct((b * l, ssd_w), F32),
                   jax.ShapeDtypeStruct((b, ssd_w, SSD_STATE), F32)],
        scratch_shapes=[pltpu.VMEM((SSD_CHUNK + 8, ssd_w), F32),
                        pltpu.VMEM((SSD_CHUNK + 8, ws), F32),
                        pltpu.VMEM((SSD_CHUNK + 8, ws), F32),
                        pltpu.VMEM((ssd_w, SSD_STATE), F32)],
        compiler_params=_cparams("arbitrary", "arbitrary"),
        name="ssd_mixer",
    )(proj, proj, proj, proj, proj_dt, conv0, ssm0, conv_w, conv_b.reshape(1, cch),
      hpad(dt_bias), hpad(a_log), jnp.repeat(d_skip, SSD_HEAD_DIM).reshape(1, ssd_w),
      norm_w.reshape(1, ssd_w), exp_h, exp_r)


def _out_kernel(attn_ref, ssd_ref, x_ref, wa_ref, ws_ref, npost_ref, gate_ref, npre_ref, sc_ref, sh_ref,
                x1_ref, h2_ref):
    mix = _dot(attn_ref[...].astype(BF16), wa_ref[...]) + _dot(ssd_ref[...].astype(BF16), ws_ref[...])
    x1 = x_ref[...] + gate_ref[...] * _rms(mix, npost_ref[...])
    x1_ref[...] = x1
    h2_ref[...] = (_rms(x1, npre_ref[...]) * (1.0 + sc_ref[...]) + sh_ref[...]).astype(BF16)


def _out_proj(attn, ssd, x, w_attn, w_ssd, n_post, gate, n_pre, scale, shift, tm):
    t, d = x.shape
    aw, sw = attn.shape[1], ssd.shape[1]
    mr = gate.shape[0]
    mod_spec = (pl.BlockSpec((1, d), lambda i: (0, 0)) if mr == 1 else pl.BlockSpec((tm, d), lambda i: (i, 0)))
    vec = pl.BlockSpec((1, d), lambda i: (0, 0))
    return pl.pallas_call(
        _out_kernel,
        grid=(t // tm,),
        in_specs=[pl.BlockSpec((tm, aw), lambda i: (i, 0)),
                  pl.BlockSpec((tm, sw), lambda i: (i, 0)),
                  pl.BlockSpec((tm, d), lambda i: (i, 0)),
                  pl.BlockSpec((aw, d), lambda i: (0, 0)),
                  pl.BlockSpec((sw, d), lambda i: (0, 0)),
                  vec, mod_spec, vec, mod_spec, mod_spec],
        out_specs=[pl.BlockSpec((tm, d), lambda i: (i, 0)),
                   pl.BlockSpec((tm, d), lambda i: (i, 0))],
        out_shape=[jax.ShapeDtypeStruct((t, d), F32), jax.ShapeDtypeStruct((t, d), BF16)],
        compiler_params=_cparams("arbitrary"),
        name="out_proj",
    )(attn, ssd, x, w_attn, w_ssd, n_post.reshape(1, d), gate, n_pre.reshape(1, d), scale, shift)


def _route_kernel(h_ref, wq_ref, sk_ref, a_ref, b_ref, g_ref):
    tt = h_ref.shape[0]
    kd = sk_ref.shape[2]
    q = _dot(h_ref[...], wq_ref[...])
    key_iota = lax.broadcasted_iota(jnp.int32, (PEER_KEYS, tt), 0)
    k8 = PEER_TOPK // 2
    row8 = lax.broadcasted_iota(jnp.int32, (k8, tt), 0)
    row16 = lax.broadcasted_iota(jnp.int32, (PEER_TOPK, tt), 0)
    flat = [row16 * PEER_TOPK] + [row8 * PEER_TOPK + kb for kb in range(1, k8)] + [row8 + k8]
    cand_flat = jnp.concatenate(flat, axis=0)
    a_rows, b_rows, g_rows = [], [], []
    for h in range(PEER_HEADS):
        tops = []
        for s in range(2):
            qhs = q[:, (2 * h + s) * kd:(2 * h + s + 1) * kd]
            sc_t = _dot3(sk_ref[s], qhs, _dot_nt)
            tops.append(_topk_rows(sc_t, PEER_TOPK, key_iota))
        (s1, i1), (s2, i2) = tops
        pieces = [s1 + s2[0:1, :]]
        for kb in range(1, k8):
            pieces.append(jnp.where(row8 < PEER_TOPK // (kb + 1), s1[0:k8, :] + s2[kb:kb + 1, :], -jnp.inf))
        pieces.append(s1[0:1, :] + s2[k8:, :])
        top, pos = _topk_rows(jnp.concatenate(pieces, axis=0), PEER_TOPK, cand_flat)
        pa, pb = pos >> 4, pos & (PEER_TOPK - 1)
        ai = jnp.zeros_like(pos)
        bi = jnp.zeros_like(pos)
        for kk in range(PEER_TOPK):
            ai = jnp.where(pa == kk, i1[kk:kk + 1, :], ai)
            bi = jnp.where(pb == kk, i2[kk:kk + 1, :], bi)
        e = jnp.exp(top - jnp.max(top, axis=0, keepdims=True))
        g_rows.append(e / jnp.sum(e, axis=0, keepdims=True))
        a_rows.append(ai)
        b_rows.append(bi)
    a_ref[...] = jnp.concatenate(a_rows, axis=0).astype(F32).T.astype(jnp.int32)
    b_ref[...] = jnp.concatenate(b_rows, axis=0).astype(F32).T.astype(jnp.int32)
    g_ref[...] = jnp.concatenate(g_rows, axis=0).T


def _peer_route(h2, wq, sub_keys, tt):
    t, d = h2.shape
    nq = wq.shape[1]
    ns = PEER_HEADS * PEER_TOPK
    assert ns == LANES and sub_keys.shape[1] == PEER_KEYS == LANES and PEER_TOPK == 16
    tok = pl.BlockSpec((tt, ns), lambda i: (i, 0))
    return pl.pallas_call(
        _route_kernel,
        grid=(t // tt,),
        in_specs=[pl.BlockSpec((tt, d), lambda i: (i, 0)),
                  pl.BlockSpec((d, nq), lambda i: (0, 0)),
                  pl.BlockSpec(sub_keys.shape, lambda i: (0, 0, 0))],
        out_specs=[tok, tok, tok],
        out_shape=[jax.ShapeDtypeStruct((t, ns), jnp.int32), jax.ShapeDtypeStruct((t, ns), jnp.int32),
                   jax.ShapeDtypeStruct((t, ns), F32)],
        compiler_params=_cparams("arbitrary"),
        name="peer_route",
    )(h2, wq, sub_keys)


def _peer_kernel(h_ref, a_ref, b_ref, g_ref, ulo_ref, uhi_ref, vlo_ref, vhi_ref, o_ref, w_scr, *, na):
    j = pl.program_id(1)
    tt = h_ref.shape[0]
    nk = PEER_KEYS
    half = nk // 2

    @pl.when(j == 0)
    def _():
        sub = lax.broadcasted_iota(jnp.int32, (nk, LANES), 0)

        def build(tb, carry):
            for r in range(PEER_BUILD_UNROLL):
                t = tb * PEER_BUILD_UNROLL + r
                arow = a_ref[pl.ds(t, 1), :]
                brow = b_ref[pl.ds(t, 1), :]
                grow = g_ref[pl.ds(t, 1), :]
                oa = jnp.where(sub == arow, 1.0, 0.0).astype(BF16)
                zb = jnp.where(sub == brow, grow, 0.0).astype(BF16)
                w_scr[pl.ds(pl.multiple_of(t * W_PITCH, 8), nk), :] = _dot_nt(oa, zb)
            return carry

        lax.fori_loop(0, tt // PEER_BUILD_UNROLL, build, 0)

    x = h_ref[...]

    def weights(a0):
        return jnp.concatenate([w_scr[pl.ds(a0 + a, tt, stride=W_PITCH), :] for a in range(na)], axis=1)

    p_lo = (weights(j * na) * jax.nn.gelu(_dot_nt(x, ulo_ref[...]))).astype(BF16)
    p_hi = (weights(j * na + half) * jax.nn.gelu(_dot_nt(x, uhi_ref[...]))).astype(BF16)
    contrib = _dot(p_lo, vlo_ref[...]) + _dot(p_hi, vhi_ref[...])

    @pl.when(j == 0)
    def _():
        o_ref[...] = contrib

    @pl.when(j != 0)
    def _():
        o_ref[...] += contrib


def _peer_experts(h2, a_idx, b_idx, g, u, v, tt, na):
    t, d = h2.shape
    ne = u.shape[0]
    eb = na * PEER_KEYS
    nsteps = ne // (2 * eb)
    assert ne == PEER_KEYS * PEER_KEYS and nsteps * 2 * eb == ne and t % tt == 0 and tt % PEER_BUILD_UNROLL == 0
    ns = a_idx.shape[1]
    tok = pl.BlockSpec((tt, ns), lambda i, j: (i, 0))
    lo = pl.BlockSpec((eb, d), lambda i, j: (j, 0))
    hi = pl.BlockSpec((eb, d), lambda i, j: (j + nsteps, 0))
    return pl.pallas_call(
        functools.partial(_peer_kernel, na=na),
        grid=(t // tt, nsteps),
        in_specs=[pl.BlockSpec((tt, d), lambda i, j: (i, 0)), tok, tok, tok, lo, hi, lo, hi],
        out_specs=pl.BlockSpec((tt, d), lambda i, j: (i, 0)),
        out_shape=jax.ShapeDtypeStruct((t, d), F32),
        scratch_shapes=[pltpu.VMEM((tt * W_PITCH, LANES), F32)],
        compiler_params=_cparams("arbitrary", "arbitrary"),
        name="peer_experts",
    )(h2, a_idx, b_idx, g, u, u, v, v)


def _final_kernel(x_ref, f_ref, nw_ref, gate_ref, o_ref):
    o_ref[...] = x_ref[...] + gate_ref[...] * _rms(f_ref[...], nw_ref[...])


def _final(x1, ffn, row0, n_post, gate, tm):
    t, d = x1.shape
    assert row0 % tm == 0
    blk0 = row0 // tm
    mr = gate.shape[0]
    mod_spec = (pl.BlockSpec((1, d), lambda i: (0, 0)) if mr == 1 else pl.BlockSpec((tm, d), lambda i: (i, 0)))
    return pl.pallas_call(
        _final_kernel,
        grid=(t // tm,),
        in_specs=[pl.BlockSpec((tm, d), lambda i: (i, 0)), pl.BlockSpec((tm, d), lambda i: (i + blk0, 0)),
                  pl.BlockSpec((1, d), lambda i: (0, 0)), mod_spec],
        out_specs=pl.BlockSpec((tm, d), lambda i: (i, 0)),
        out_shape=jax.ShapeDtypeStruct((t, d), F32),
        compiler_params=_cparams("arbitrary"),
        name="final_residual",
    )(x1, ffn, n_post.reshape(1, d), gate)


def _tile(n, pref):
    t = min(n, pref)
    while n % t:
        t //= 2
    return t


def kernel(x_prompt, x_sample, c_prompt, c_sample, cache_k, cache_v, page_table, state_conv, state_ssm,
           w_ada, b_ada, norm_mix_pre, norm_mix_post, norm_ffn_pre, norm_ffn_post, w_in, conv_w, conv_b,
           dt_bias, a_log, d_skip, ssd_norm_w, w_out, peer_w_query, peer_sub_keys, peer_u, peer_v):
    bp, lp, d = x_prompt.shape
    bs, ls, _ = x_sample.shape
    assert bp == 1 and lp % MOBA_BLOCK == 0 and ls <= SSD_CHUNK
    attn_w = d // 2
    ssd_w = d - attn_w
    nh_a = attn_w // HEAD_DIM
    nh_s = ssd_w // SSD_HEAD_DIM
    cch = conv_w.shape[1]
    n_main = 3 * attn_w + ssd_w + cch
    n_pool, page, _, _ = cache_k.shape
    tp, ts = bp * lp, bs * ls

    rows = bp + bs
    rpad = -rows % 8
    c_all = jnp.pad(jnp.concatenate([c_prompt, c_sample], axis=0), ((0, rpad), (0, 0)))
    mod = _modulation(c_all, w_ada, b_ada)
    mod_p = [mod[0:bp, k * d:(k + 1) * d] for k in range(6)]
    mod_s = [jnp.repeat(mod[bp:rows, k * d:(k + 1) * d], ls, axis=0) for k in range(6)]

    w_in_t = w_in.T
    w_main = w_in_t[:n_main].astype(BF16)
    w_dt = jnp.pad(w_in_t[n_main:], ((0, LANES - nh_s), (0, 0))).astype(BF16)
    w_attn = w_out[:attn_w].astype(BF16)
    w_ssd = w_out[attn_w:].astype(BF16)
    wq = peer_w_query.astype(BF16)
    u_b = peer_u.astype(BF16)
    v_b = peer_v.astype(BF16)

    xp = x_prompt.reshape(tp, d)
    xs = x_sample.reshape(ts, d)
    tn = _tile(n_main, 512)
    proj_p, dt_p = _in_proj(xp, norm_mix_pre, mod_p[1], mod_p[0], w_main, w_dt, _tile(tp, 1024), tn)
    proj_s, dt_s = _in_proj(xs, norm_mix_pre, mod_s[1], mod_s[0], w_main, w_dt, _tile(ts, 256),
                            _tile(n_main, n_main // 2))

    attn_p = _moba_prompt(proj_p, lp, attn_w)
    ck_t = jnp.transpose(cache_k, (0, 2, 3, 1)).reshape(n_pool, attn_w, page)
    cv_t = jnp.transpose(cache_v, (0, 2, 3, 1)).reshape(n_pool, attn_w, page)
    attn_s = _moba_sample(proj_s.reshape(bs, ls, n_main), ck_t, cv_t, page_table, attn_w).reshape(ts, attn_w)

    conv0_p = jnp.zeros((bp, SSD_CONV - 1, cch), F32)
    ssm0_p = jnp.zeros((bp, ssd_w, SSD_STATE), F32)
    ssd_args = (conv_w, conv_b, dt_bias, a_log, d_skip, ssd_norm_w)
    ssd_p, ssm_p = _ssd(proj_p, dt_p, conv0_p, ssm0_p, *ssd_args, bp, lp, ssd_w)
    ssd_s, ssm_s = _ssd(proj_s, dt_s, state_conv, state_ssm.reshape(bs, ssd_w, SSD_STATE), *ssd_args,
                        bs, ls, ssd_w)

    x1_p, h2_p = _out_proj(attn_p, ssd_p, xp, w_attn, w_ssd, norm_mix_post, mod_p[2], norm_ffn_pre,
                           mod_p[4], mod_p[3], _tile(tp, 512))
    x1_s, h2_s = _out_proj(attn_s, ssd_s, xs, w_attn, w_ssd, norm_mix_post, mod_s[2], norm_ffn_pre,
                           mod_s[4], mod_s[3], _tile(ts, 256))

    h2 = jnp.concatenate([h2_p, h2_s], axis=0)
    a_idx, b_idx, g = _peer_route(h2, wq, peer_sub_keys, _tile(tp + ts, 256))
    tt_e = PEER_TILES[0] if (tp + ts) % PEER_TILES[0] == 0 else _tile(tp + ts, 256)
    ffn = _peer_experts(h2, a_idx, b_idx, g, u_b, v_b, tt_e, PEER_TILES[1])

    y_p = _final(x1_p, ffn, 0, norm_ffn_post, mod_p[5], _tile(tp, 512))
    y_s = _final(x1_s, ffn, tp, norm_ffn_post, mod_s[5], _tile(ts, 256))

    kcol, vcol, xcol = attn_w, 2 * attn_w, 3 * attn_w + ssd_w
    k_p = proj_p[:, kcol:kcol + attn_w].reshape(bp, lp, nh_a, HEAD_DIM)
    v_p = proj_p[:, vcol:vcol + attn_w].reshape(bp, lp, nh_a, HEAD_DIM)
    conv_p = proj_p.reshape(bp, lp, n_main)[:, lp - (SSD_CONV - 1):, xcol:xcol + cch]
    k_s = proj_s[:, kcol:kcol + attn_w].reshape(bs, ls, nh_a, HEAD_DIM)
    v_s = proj_s[:, vcol:vcol + attn_w].reshape(bs, ls, nh_a, HEAD_DIM)
    tail = min(ls, SSD_CONV - 1)
    xbc_s = proj_s.reshape(bs, ls, n_main)[:, ls - tail:, xcol:xcol + cch]
    conv_s = jnp.concatenate([state_conv[:, ls:], xbc_s], axis=1)
    return (y_p.reshape(bp, lp, d), y_s.reshape(bs, ls, d), k_p, v_p, conv_p,
            ssm_p.reshape(bp, nh_s, SSD_HEAD_DIM, SSD_STATE), k_s, v_s, conv_s,
            ssm_s.reshape(bs, nh_s, SSD_HEAD_DIM, SSD_STATE))
```

```python
import functools

import jax
import jax.numpy as jnp
from jax import lax
from jax.experimental import pallas as pl
from jax.experimental.pallas import tpu as pltpu

F32 = jnp.float32
BF16 = jnp.bfloat16

HEAD_DIM = 64
MOBA_BLOCK = 256
MOBA_TOPK = 3
SSD_HEAD_DIM = 64
SSD_GROUPS = 2
SSD_STATE = 128
SSD_CONV = 4
SSD_CHUNK = 128
PEER_HEADS = 8
PEER_KEYS = 128
PEER_TOPK = 16
RMS_EPS = 1e-6

LANES = 128
NEG = -1e30
LOG2E = 1.4426950408889634
VMEM_LIMIT = 56 * 1024 * 1024
MOBA_UNROLL = 2
MOBA_QBLOCKS = 4
W_PITCH = 136
PEER_BUILD_UNROLL = 32
PEER_TILES = (384, 4)
SAMPLE_BLOCKS = 8


def _cparams(*sem):
    return pltpu.CompilerParams(dimension_semantics=sem, vmem_limit_bytes=VMEM_LIMIT)


def _dot(a, b):
    return jnp.dot(a, b, preferred_element_type=F32)


def _dot_nt(a, b):
    return lax.dot_general(a, b, (((1,), (1,)), ((), ())), preferred_element_type=F32)


def _dot_tn(a, b):
    return lax.dot_general(a, b, (((0,), (0,)), ((), ())), preferred_element_type=F32)


def _split2(x):
    hi = x.astype(BF16)
    lo = (x - hi.astype(F32)).astype(BF16)
    return hi, lo


def _split3(x):
    hi = x.astype(BF16)
    r = x - hi.astype(F32)
    mid = r.astype(BF16)
    lo = (r - mid.astype(F32)).astype(BF16)
    return hi, mid, lo


def _dot3(a, b, dot=_dot):
    ah, al = _split2(a)
    bh, bl = _split2(b)
    return dot(ah, bh) + (dot(ah, bl) + dot(al, bh))


def _dot_exact_rhs(a, b_bf16, dot=_dot):
    h, m, l = _split3(a)
    return dot(h, b_bf16) + (dot(m, b_bf16) + dot(l, b_bf16))


def _silu(x):
    return x * jax.nn.sigmoid(x)


def _rms(x, w):
    return x * lax.rsqrt(jnp.mean(x * x, axis=-1, keepdims=True) + RMS_EPS) * w


def _topk_rows(cur, k, iota):
    vals, idxs = [], []
    for _ in range(k):
        m = jnp.max(cur, axis=0, keepdims=True)
        idx = jnp.min(jnp.where(cur == m, iota, jnp.iinfo(jnp.int32).max), axis=0, keepdims=True)
        vals.append(m)
        idxs.append(idx)
        cur = jnp.where(iota == idx, -jnp.inf, cur)
    return jnp.concatenate(vals, axis=0), jnp.concatenate(idxs, axis=0)


def _mod_kernel(c_ref, w_ref, b_ref, o_ref):
    s = _silu(c_ref[...]).astype(BF16)
    o_ref[...] = _dot(s, w_ref[...].astype(BF16)) + b_ref[...]


def _modulation(c_all, w_ada, b_ada):
    r, d = c_all.shape
    n = w_ada.shape[1]
    tn = 1536 if n % 1536 == 0 else n
    return pl.pallas_call(
        _mod_kernel,
        grid=(n // tn,),
        in_specs=[pl.BlockSpec((r, d), lambda j: (0, 0)),
                  pl.BlockSpec((d, tn), lambda j: (0, j)),
                  pl.BlockSpec((1, tn), lambda j: (0, j))],
        out_specs=pl.BlockSpec((r, tn), lambda j: (0, j)),
        out_shape=jax.ShapeDtypeStruct((r, n), F32),
        compiler_params=_cparams("arbitrary"),
        name="adaln_mod",
    )(c_all, w_ada, b_ada.reshape(1, n))


def _in_kernel(x_ref, nw_ref, sc_ref, sh_ref, w_ref, wdt_ref, o_ref, odt_ref, h_scr):
    @pl.when(pl.program_id(1) == 0)
    def _():
        h = _rms(x_ref[...], nw_ref[...]) * (1.0 + sc_ref[...]) + sh_ref[...]
        h_scr[...] = h.astype(BF16)
        odt_ref[...] = _dot_nt(h_scr[...], wdt_ref[...])

    o_ref[...] = _dot_nt(h_scr[...], w_ref[...])


def _in_proj(x, norm_w, scale, shift, w_main, w_dt, tm, tn):
    t, d = x.shape
    n = w_main.shape[0]
    mr = scale.shape[0]
    mod_spec = (pl.BlockSpec((1, d), lambda i, j: (0, 0)) if mr == 1
                else pl.BlockSpec((tm, d), lambda i, j: (i, 0)))
    return pl.pallas_call(
        _in_kernel,
        grid=(t // tm, n // tn),
        in_specs=[pl.BlockSpec((tm, d), lambda i, j: (i, 0)),
                  pl.BlockSpec((1, d), lambda i, j: (0, 0)),
                  mod_spec, mod_spec,
                  pl.BlockSpec((tn, d), lambda i, j: (j, 0)),
                  pl.BlockSpec((LANES, d), lambda i, j: (0, 0))],
        out_specs=[pl.BlockSpec((tm, tn), lambda i, j: (i, j)),
                   pl.BlockSpec((tm, LANES), lambda i, j: (i, 0))],
        out_shape=[jax.ShapeDtypeStruct((t, n), F32),
                   jax.ShapeDtypeStruct((t, LANES), F32)],
        scratch_shapes=[pltpu.VMEM((tm, d), BF16)],
        compiler_params=_cparams("arbitrary", "arbitrary"),
        name="in_proj",
    )(x, norm_w.reshape(1, d), scale, shift, w_main, w_dt)


def _moba_prompt_kernel(q_ref, k_ref, v_ref, o_ref, kb_scr, vt_scr, kmean_scr, sel_scr, sc_scr, pr_scr, *, nb):
    g = pl.program_id(1)
    blk = MOBA_BLOCK
    hd = HEAD_DIM

    @pl.when(g == 0)
    def _():
        row = lax.broadcasted_iota(jnp.int32, (2 * hd, blk), 0)
        ones_row = jnp.where(row == hd, 1.0, 0.0)
        for n in range(nb):
            kn = k_ref[pl.ds(n * blk, blk), :]
            kmean_scr[pl.ds(n, 1), :] = jnp.sum(kn, axis=0, keepdims=True) * (1.0 / blk)
            kb_scr[pl.ds(n * blk, blk), :] = kn.astype(BF16)
            vt = v_ref[pl.ds(n * blk, blk), :].T
            vt_scr[0, :, pl.ds(n * blk, blk)] = jnp.where(row < hd, vt, ones_row).astype(BF16)
            vt_sw = jnp.concatenate([vt[hd:], vt[:hd]], axis=0)
            vt_scr[1, :, pl.ds(n * blk, blk)] = jnp.where(row < hd, vt_sw, ones_row).astype(BF16)

    lane_head = lax.broadcasted_iota(jnp.int32, (blk, LANES), 1) // hd
    blk_iota = lax.broadcasted_iota(jnp.int32, (nb, blk), 0)
    key_i = lax.broadcasted_iota(jnp.int32, (blk, blk), 0)
    qry_i = lax.broadcasted_iota(jnp.int32, (blk, blk), 1)
    scale = hd ** -0.5
    nq = MOBA_QBLOCKS
    chains = [(c, hh) for c in range(nq) for hh in range(2)]
    i_last = nq * g + nq - 1

    qs = []
    for ch, (c, hh) in enumerate(chains):
        i = nq * g + c
        qm = jnp.where(lane_head == hh, q_ref[pl.ds(c * blk, blk), :], 0.0)
        s_blk = _dot3(kmean_scr[...], qm, _dot_nt)
        cur = jnp.where(blk_iota < i, s_blk, -jnp.inf)
        sel = jnp.zeros((nb, blk), F32)
        for s in range(min(MOBA_TOPK, nb)):
            m = jnp.max(cur, axis=0, keepdims=True)
            idx = jnp.min(jnp.where(cur == m, blk_iota, nb), axis=0, keepdims=True)
            hit = blk_iota == idx
            sel = jnp.where(hit, jnp.where(i > s, 1.0, sel), sel)
            cur = jnp.where(hit, -jnp.inf, cur)
        sel_scr[ch] = sel
        qs.append((qm * (scale * LOG2E)).astype(BF16))

    span = MOBA_UNROLL * blk

    def scores(off, width, which):
        kb = kb_scr[pl.ds(off, width), :]
        return [_dot_nt(kb, qs[ch]) for ch in which]

    def softmax_step(s_t, masks, m_run):
        s_t = jnp.concatenate([jnp.where(mk, s_t[u * blk:(u + 1) * blk], NEG) for u, mk in enumerate(masks)], axis=0)
        m_new = jnp.maximum(m_run, jnp.max(s_t, axis=0, keepdims=True))
        return m_new, jnp.exp2(m_run - m_new), jnp.exp2(s_t - m_new).astype(BF16)

    def accumulate(hh, off, width, alpha, p, acc):
        return alpha * acc + _dot(vt_scr[hh, :, pl.ds(off, width)], p)

    nsteps = (i_last + MOBA_UNROLL - 1) // MOBA_UNROLL
    max_step = nb // MOBA_UNROLL - 1
    every = list(range(len(chains)))
    for ch, s0 in enumerate(scores(0, span, every)):
        sc_scr[0, ch] = s0
        pr_scr[1, ch] = jnp.zeros((span, blk), BF16)

    def body(k, carry):
        m, acc, alpha_prev = carry
        for cur in range(2):
            nxt = 1 - cur
            t = 2 * k + cur
            live = t < nsteps
            s_next = scores(pl.multiple_of(jnp.minimum(t + 1, max_step) * span, span), span, every)
            off_prev = pl.multiple_of(jnp.maximum(t - 1, 0) * span, span)
            tm = jnp.minimum(t, max_step)
            m_out, acc_out, alpha_out = [], [], []
            for ch, (c, hh) in enumerate(chains):
                sc_scr[nxt, ch] = s_next[ch]
                masks = [(sel_scr[ch, pl.ds(tm * MOBA_UNROLL + u, 1), :] > 0.5) & live for u in range(MOBA_UNROLL)]
                m_new, alpha, p = softmax_step(sc_scr[cur, ch], masks, m[ch])
                acc_out.append(accumulate(hh, off_prev, span, alpha_prev[ch], pr_scr[nxt, ch], acc[ch]))
                pr_scr[cur, ch] = p
                m_out.append(m_new)
                alpha_out.append(alpha)
            m, acc, alpha_prev = m_out, acc_out, alpha_out
        return m, acc, alpha_prev

    zero_row = jnp.zeros((1, blk), F32)
    nch = len(chains)
    init = ([zero_row + NEG] * nch, [jnp.zeros((2 * hd, blk), F32)] * nch, [zero_row + 1.0] * nch)
    ntrips = (nsteps + 1) // 2
    m, acc, alpha_prev = lax.fori_loop(0, ntrips, body, init)
    off_last = pl.multiple_of(jnp.clip(2 * ntrips - 1, 0, max_step) * span, span)
    for c in range(nq):
        off_own = pl.multiple_of((nq * g + c) * blk, blk)
        mine = [ch for ch, (cc, _) in enumerate(chains) if cc == c]
        s_own = scores(off_own, blk, mine)
        outs = []
        for ch, s_o in zip(mine, s_own):
            hh = chains[ch][1]
            a = accumulate(hh, off_last, span, alpha_prev[ch], pr_scr[1, ch], acc[ch])
            _, alpha, p = softmax_step(s_o, [key_i <= qry_i], m[ch])
            a = accumulate(hh, off_own, blk, alpha, p, a)
            outs.append(a[:hd] / a[hd:hd + 1])
        o_ref[pl.ds(c * blk, blk), :] = jnp.concatenate(outs, axis=0).T


def _moba_prompt(proj, t, attn_w):
    nb = t // MOBA_BLOCK
    assert nb % MOBA_UNROLL == 0 and nb % MOBA_QBLOCKS == 0
    npair = attn_w // LANES
    kcol = attn_w // LANES
    qrows = MOBA_QBLOCKS * MOBA_BLOCK
    nch = 2 * MOBA_QBLOCKS
    return pl.pallas_call(
        functools.partial(_moba_prompt_kernel, nb=nb),
        grid=(npair, nb // MOBA_QBLOCKS),
        in_specs=[pl.BlockSpec((qrows, LANES), lambda p, i: (i, p)),
                  pl.BlockSpec((t, LANES), lambda p, i: (0, kcol + p)),
                  pl.BlockSpec((t, LANES), lambda p, i: (0, 2 * kcol + p))],
        out_specs=pl.BlockSpec((qrows, LANES), lambda p, i: (i, p)),
        out_shape=jax.ShapeDtypeStruct((t, attn_w), F32),
        scratch_shapes=[pltpu.VMEM((t, LANES), BF16),
                        pltpu.VMEM((2, LANES, t), BF16),
                        pltpu.VMEM((nb, LANES), F32),
                        pltpu.VMEM((nch, nb, MOBA_BLOCK), F32),
                        pltpu.VMEM((2, nch, MOBA_UNROLL * MOBA_BLOCK, MOBA_BLOCK), F32),
                        pltpu.VMEM((2, nch, MOBA_UNROLL * MOBA_BLOCK, MOBA_BLOCK), BF16)],
        compiler_params=_cparams("arbitrary", "arbitrary"),
        name="moba_prompt",
    )(proj, proj, proj)


def _moba_sample_kernel(pt_ref, q_ref, kn_ref, vn_ref, *rest, nb, nh, lq, pb):
    npg = 2 * pb
    k_refs, v_refs = rest[:npg], rest[npg:2 * npg]
    o_ref, q2_scr, s_scr, mean_scr, max_scr, sel_scr, own_scr, m_scr, l_scr, acc_scr = rest[2 * npg:]
    ph = pl.program_id(1)
    n = pl.program_id(2)
    last = pl.num_programs(2) - 1
    r = nh * lq
    w = nh * HEAD_DIM
    blk = MOBA_BLOCK
    scale = HEAD_DIM ** -0.5
    row_head = lax.broadcasted_iota(jnp.int32, (r, w), 0) // lq
    col_head = lax.broadcasted_iota(jnp.int32, (r, w), 1) // HEAD_DIM
    lane = lax.broadcasted_iota(jnp.int32, (r, LANES), 1)
    own_ok = (lane < lq) & (lane <= lax.broadcasted_iota(jnp.int32, (r, LANES), 0) % lq)

    def block_t(refs, u):
        return jnp.concatenate([refs[2 * u][0], refs[2 * u + 1][0]], axis=1).astype(BF16)

    @pl.when((ph == 0) & (n == 0))
    def _():
        q = q_ref[0]
        qbd = jnp.where(row_head == col_head, jnp.concatenate([q] * nh, axis=0), 0.0)
        hi, lo = _split2(qbd)
        q2_scr[...] = jnp.concatenate([hi, lo], axis=0)
        mean_scr[...] = jnp.zeros(mean_scr.shape, F32)
        max_scr[...] = jnp.zeros(max_scr.shape, F32)

    @pl.when(ph == 0)
    def _():
        means, maxes = mean_scr[...], max_scr[...]
        for u in range(pb):
            s2 = _dot(q2_scr[...], block_t(k_refs, u))
            s = s2[:r] + s2[r:]
            nu = n * pb + u
            s_scr[:, pl.ds(pl.multiple_of(nu * blk, blk), blk)] = s
            means = jnp.where(lane == nu, jnp.sum(s, axis=1, keepdims=True) * (1.0 / blk), means)
            maxes = jnp.where(lane == nu, jnp.max(s, axis=1, keepdims=True), maxes)
        mean_scr[...] = means
        max_scr[...] = maxes

    @pl.when((ph == 0) & (n == last))
    def _():
        cur = jnp.where(lane < nb, mean_scr[...], -jnp.inf)
        sel = jnp.zeros((r, LANES), F32)
        for _ in range(MOBA_TOPK):
            mx = jnp.max(cur, axis=1, keepdims=True)
            idx = jnp.min(jnp.where(cur == mx, lane, LANES), axis=1, keepdims=True)
            hit = lane == idx
            sel = jnp.where(hit, 1.0, sel)
            cur = jnp.where(hit, -jnp.inf, cur)
        sel_scr[...] = sel
        kn = jnp.concatenate([kn_ref[0], jnp.zeros((LANES - lq, w), F32)], axis=0).astype(BF16)
        o2 = _dot_nt(q2_scr[...], kn)
        own = jnp.where(own_ok, o2[:r] + o2[r:], NEG)
        own_scr[...] = own
        m_sel = jnp.max(jnp.where(sel > 0.5, max_scr[...], NEG), axis=1, keepdims=True)
        m_scr[...] = jnp.maximum(m_sel, jnp.max(own, axis=1, keepdims=True))
        l_scr[...] = jnp.zeros(l_scr.shape, F32)
        acc_scr[...] = jnp.zeros(acc_scr.shape, F32)

    @pl.when(ph == 1)
    def _():
        l_add = jnp.zeros((r, 1), F32)
        acc_add = jnp.zeros((r, w), F32)
        for u in range(pb):
            nu = n * pb + u
            picked = jnp.sum(jnp.where(lane == nu, sel_scr[...], 0.0), axis=1, keepdims=True) > 0.5
            s = s_scr[:, pl.ds(pl.multiple_of(nu * blk, blk), blk)]
            p = jnp.where(picked, jnp.exp((s - m_scr[...]) * scale), 0.0)
            l_add = l_add + jnp.sum(p, axis=1, keepdims=True)
            acc_add = acc_add + _dot_nt(p.astype(BF16), block_t(v_refs, u))
        l_scr[...] += l_add
        acc_scr[...] += acc_add

    @pl.when((ph == 1) & (n == last))
    def _():
        p = jnp.where(own_ok, jnp.exp((own_scr[...] - m_scr[...]) * scale), 0.0)
        vn = jnp.concatenate([vn_ref[0], jnp.zeros((LANES - lq, w), F32)], axis=0).astype(BF16)
        l_fin = l_scr[...] + jnp.sum(p, axis=1, keepdims=True)
        acc = acc_scr[...] + _dot(p.astype(BF16), vn)
        full = jnp.where(row_head == col_head, acc / l_fin, 0.0)
        out = full[0:lq]
        for h in range(1, nh):
            out = out + full[h * lq:(h + 1) * lq]
        o_ref[0] = out


def _moba_sample(proj3, cache_kt, cache_vt, page_table, attn_w):
    b, lq, _ = proj3.shape
    _, w, page = cache_kt.shape
    n_pages = page_table.shape[1]
    nh = attn_w // HEAD_DIM
    r = nh * lq
    assert 2 * page == MOBA_BLOCK and page == LANES and w == attn_w and lq <= LANES and r % 8 == 0
    nb = n_pages // 2
    assert nb * 2 == n_pages and MOBA_TOPK <= nb <= LANES
    pb = SAMPLE_BLOCKS if nb % SAMPLE_BLOCKS == 0 else 1
    nsteps = nb // pb
    npg = 2 * pb
    kmap = lambda o: (lambda bi, ph, n, pt: (pt[bi, npg * (n * (1 - ph) + (nsteps - 1) * ph) + o], 0, 0))
    vmap = lambda o: (lambda bi, ph, n, pt: (pt[bi, npg * (n * ph) + o], 0, 0))
    newspec = lambda c: pl.BlockSpec((1, lq, attn_w), lambda bi, ph, n, pt: (bi, 0, c))
    pages = ([pl.BlockSpec((1, w, page), kmap(o)) for o in range(npg)]
             + [pl.BlockSpec((1, w, page), vmap(o)) for o in range(npg)])
    return pl.pallas_call(
        functools.partial(_moba_sample_kernel, nb=nb, nh=nh, lq=lq, pb=pb),
        grid_spec=pltpu.PrefetchScalarGridSpec(
            num_scalar_prefetch=1,
            grid=(b, 2, nsteps),
            in_specs=[newspec(0), newspec(1), newspec(2)] + pages,
            out_specs=pl.BlockSpec((1, lq, attn_w), lambda bi, ph, n, pt: (bi, 0, 0)),
            scratch_shapes=[pltpu.VMEM((2 * r, w), BF16),
                            pltpu.VMEM((r, nb * MOBA_BLOCK), F32),
                            pltpu.VMEM((r, LANES), F32),
                            pltpu.VMEM((r, LANES), F32),
                            pltpu.VMEM((r, LANES), F32),
                            pltpu.VMEM((r, LANES), F32),
                            pltpu.VMEM((r, 1), F32),
                            pltpu.VMEM((r, 1), F32),
                            pltpu.VMEM((r, w), F32)]),
        out_shape=jax.ShapeDtypeStruct((b, lq, attn_w), F32),
        compiler_params=_cparams("arbitrary", "arbitrary", "arbitrary"),
        name="moba_sample",
    )(page_table, proj3, proj3, proj3, *([cache_kt] * npg), *([cache_vt] * npg))


def _ssd_kernel(z_ref, xs_ref, bm_ref, cm_ref, dt_ref, conv0_ref, ssm0_ref,
                cw_ref, cb_ref, dtb_ref, alog_ref, dskip_ref, nw_ref, exp_h_ref, exp_r_ref,
                y_ref, ssm_ref, bx_scr, bb_scr, bc_scr, st_scr, *, lv, nheads):
    c = pl.program_id(1)
    q = SSD_CHUNK
    wx = xs_ref.shape[-1]
    ws = bm_ref.shape[-1]
    pad = 8
    npair = wx // LANES
    rep = nheads // SSD_GROUPS

    @pl.when(c == 0)
    def _():
        bx_scr[...] = jnp.zeros(bx_scr.shape, F32)
        bb_scr[...] = jnp.zeros(bb_scr.shape, F32)
        bc_scr[...] = jnp.zeros(bc_scr.shape, F32)
        c0 = conv0_ref[0]
        lo = pad - (SSD_CONV - 1)
        bx_scr[lo:pad, :] = c0[:, :wx]
        bb_scr[lo:pad, :] = c0[:, wx:wx + ws]
        bc_scr[lo:pad, :] = c0[:, wx + ws:]
        st_scr[...] = ssm0_ref[0]

    def conv(src_ref, buf, c_lo, c_hi):
        buf[pad:pad + lv, :] = src_ref[...]
        acc = cb_ref[:, c_lo:c_hi]
        for i in range(SSD_CONV):
            acc = acc + buf[pl.ds(pad - (SSD_CONV - 1) + i, q), :] * cw_ref[i:i + 1, c_lo:c_hi]
        buf[pad - (SSD_CONV - 1):pad, :] = buf[pad + lv - (SSD_CONV - 1):pad + lv, :]
        return _silu(acc)

    xs = conv(xs_ref, bx_scr, 0, wx)
    bm = conv(bm_ref, bb_scr, wx, wx + ws)
    cm = conv(cm_ref, bc_scr, wx + ws, wx + 2 * ws)

    row = lax.broadcasted_iota(jnp.int32, (q, LANES), 0)
    if lv == q:
        dt_raw = dt_ref[...]
    else:
        dt_raw = jnp.concatenate([dt_ref[...], jnp.zeros((q - lv, LANES), F32)], axis=0)
    dt = jnp.where(row < lv, jax.nn.softplus(dt_raw + dtb_ref[...]), 0.0)
    da = dt * (-jnp.exp(alog_ref[...]))

    li = lax.broadcasted_iota(jnp.int32, (q, q), 0)
    si = lax.broadcasted_iota(jnp.int32, (q, q), 1)
    tril = li >= si
    tri_l = jnp.where(tril, 1.0, 0.0).astype(BF16)
    tri_u = jnp.where(li <= si, 1.0, 0.0).astype(BF16)
    cs = _dot_exact_rhs(da, tri_l, lambda a, t: _dot(t, a))
    cs_t = _dot_exact_rhs(da, tri_u, _dot_tn)
    exp_h = exp_h_ref[...]
    dt_full = _dot_exact_rhs(dt, exp_h)
    cs_full = _dot_exact_rhs(cs, exp_h)
    cs_rep = _dot_exact_rhs(cs, exp_r_ref[...])
    cs_last = cs_full[q - 1:q, :]
    xdt = xs * dt_full
    xdec = (xdt * jnp.exp(cs_last - cs_full)).astype(BF16)
    exp_cs = jnp.exp(cs_full)
    xdt_b = xdt.astype(BF16)
    lane_half = lax.broadcasted_iota(jnp.int32, (q, LANES), 1) // SSD_HEAD_DIM
    rowh = lax.broadcasted_iota(jnp.int32, (LANES, 1), 0) // SSD_HEAD_DIM
    last_t = cs_t[:, q - 1:q]

    g_scores = []
    for g in range(SSD_GROUPS):
        cg = cm[:, g * SSD_STATE:(g + 1) * SSD_STATE].astype(BF16)
        bg = bm[:, g * SSD_STATE:(g + 1) * SSD_STATE].astype(BF16)
        g_scores.append((cg, bg, _dot_nt(cg, bg)))

    ys = []
    for p in range(npair):
        h0 = 2 * p
        cg, bg, sc = g_scores[h0 // rep]
        sl = slice(p * LANES, (p + 1) * LANES)
        xp = xdt_b[:, sl]
        yd = jnp.zeros((q, LANES), F32)
        for hh in range(2):
            h = h0 + hh
            diff = cs_rep[:, h * LANES:h * LANES + q] - cs_t[h:h + 1, :]
            lmat = jnp.exp(jnp.where(tril, diff, NEG))
            mh = (sc * lmat).astype(BF16)
            yd = yd + _dot(mh, jnp.where(lane_half == hh, xp, jnp.zeros_like(xp)))
        st = st_scr[sl, :]
        y_off = _dot_nt(cg, st.astype(BF16)) * exp_cs[:, sl]
        ys.append(yd + y_off)
        dec_col = jnp.where(rowh == 0, jnp.exp(last_t[h0:h0 + 1, :]), jnp.exp(last_t[h0 + 1:h0 + 2, :]))
        st_scr[sl, :] = dec_col * st + _dot_tn(xdec[:, sl], bg)

    y = jnp.concatenate(ys, axis=1) + xs * dskip_ref[...]
    y_ref[...] = _rms(y[:lv] * _silu(z_ref[...]), nw_ref[...])
    ssm_ref[0] = st_scr[...]


def _ssd(proj, proj_dt, conv0, ssm0, conv_w, conv_b, dt_bias, a_log, d_skip, norm_w, b, l, ssd_w):
    ws = SSD_GROUPS * SSD_STATE
    nheads = ssd_w // SSD_HEAD_DIM
    assert nheads <= LANES and nheads % (2 * SSD_GROUPS) == 0 and SSD_STATE == LANES
    lv = min(l, SSD_CHUNK)
    nc = l // lv
    assert nc * lv == l
    attn_cols = proj.shape[1] - 2 * ssd_w - 2 * ws
    assert attn_cols % ssd_w == 0 and (attn_cols + 2 * ssd_w) % ws == 0
    zc = attn_cols // ssd_w
    xc = zc + 1
    bc = (attn_cols + 2 * ssd_w) // ws
    hpad = lambda v: jnp.pad(v.reshape(1, -1), ((0, 0), (0, LANES - nheads)))
    hid = jnp.arange(LANES)[:, None]
    exp_h = (hid == (jnp.arange(ssd_w)[None, :] // SSD_HEAD_DIM)).astype(BF16)
    exp_r = (hid == (jnp.arange(nheads * LANES)[None, :] // LANES)).astype(BF16)
    cch = conv_w.shape[1]
    rowblk = lambda bi, c: bi * nc + c
    const = lambda bi, c: (0, 0)
    return pl.pallas_call(
        functools.partial(_ssd_kernel, lv=lv, nheads=nheads),
        grid=(b, nc),
        in_specs=[pl.BlockSpec((lv, ssd_w), lambda bi, c: (rowblk(bi, c), zc)),
                  pl.BlockSpec((lv, ssd_w), lambda bi, c: (rowblk(bi, c), xc)),
                  pl.BlockSpec((lv, ws), lambda bi, c: (rowblk(bi, c), bc)),
                  pl.BlockSpec((lv, ws), lambda bi, c: (rowblk(bi, c), bc + 1)),
                  pl.BlockSpec((lv, LANES), lambda bi, c: (rowblk(bi, c), 0)),
                  pl.BlockSpec((1, SSD_CONV - 1, cch), lambda bi, c: (bi, 0, 0)),
                  pl.BlockSpec((1, ssd_w, SSD_STATE), lambda bi, c: (bi, 0, 0)),
                  pl.BlockSpec((SSD_CONV, cch), const),
                  pl.BlockSpec((1, cch), const),
                  pl.BlockSpec((1, LANES), const),
                  pl.BlockSpec((1, LANES), const),
                  pl.BlockSpec((1, ssd_w), const),
                  pl.BlockSpec((1, ssd_w), const),
                  pl.BlockSpec((LANES, ssd_w), const),
                  pl.BlockSpec((LANES, nheads * LANES), const)],
        out_specs=[pl.BlockSpec((lv, ssd_w), lambda bi, c: (rowblk(bi, c), 0)),
                   pl.BlockSpec((1, ssd_w, SSD_STATE), lambda bi, c: (bi, 0, 0))],
        out_shape=[jax.ShapeDtypeStruct((b * l, ssd_w), F32),
                   jax.ShapeDtypeStruct((b, ssd_w, SSD_STATE), F32)],
        scratch_shapes=[pltpu.VMEM((SSD_CHUNK + 8, ssd_w), F32),
                        pltpu.VMEM((SSD_CHUNK + 8, ws), F32),
                        pltpu.VMEM((SSD_CHUNK + 8, ws), F32),
                        pltpu.VMEM((ssd_w, SSD_STATE), F32)],
        compiler_params=_cparams("arbitrary", "arbitrary"),
        name="ssd_mixer",
    )(proj, proj, proj, proj, proj_dt, conv0, ssm0, conv_w, conv_b.reshape(1, cch),
      hpad(dt_bias), hpad(a_log), jnp.repeat(d_skip, SSD_HEAD_DIM).reshape(1, ssd_w),
      norm_w.reshape(1, ssd_w), exp_h, exp_r)


def _out_kernel(attn_ref, ssd_ref, x_ref, wa_ref, ws_ref, npost_ref, gate_ref, npre_ref, sc_ref, sh_ref,
                x1_ref, h2_ref):
    mix = _dot(attn_ref[...].astype(BF16), wa_ref[...]) + _dot(ssd_ref[...].astype(BF16), ws_ref[...])
    x1 = x_ref[...] + gate_ref[...] * _rms(mix, npost_ref[...])
    x1_ref[...] = x1
    h2_ref[...] = (_rms(x1, npre_ref[...]) * (1.0 + sc_ref[...]) + sh_ref[...]).astype(BF16)


def _out_proj(attn, ssd, x, w_attn, w_ssd, n_post, gate, n_pre, scale, shift, tm):
    t, d = x.shape
    aw, sw = attn.shape[1], ssd.shape[1]
    mr = gate.shape[0]
    mod_spec = (pl.BlockSpec((1, d), lambda i: (0, 0)) if mr == 1 else pl.BlockSpec((tm, d), lambda i: (i, 0)))
    vec = pl.BlockSpec((1, d), lambda i: (0, 0))
    return pl.pallas_call(
        _out_kernel,
        grid=(t // tm,),
        in_specs=[pl.BlockSpec((tm, aw), lambda i: (i, 0)),
                  pl.BlockSpec((tm, sw), lambda i: (i, 0)),
                  pl.BlockSpec((tm, d), lambda i: (i, 0)),
                  pl.BlockSpec((aw, d), lambda i: (0, 0)),
                  pl.BlockSpec((sw, d), lambda i: (0, 0)),
                  vec, mod_spec, vec, mod_spec, mod_spec],
        out_specs=[pl.BlockSpec((tm, d), lambda i: (i, 0)),
                   pl.BlockSpec((tm, d), lambda i: (i, 0))],
        out_shape=[jax.ShapeDtypeStruct((t, d), F32), jax.ShapeDtypeStruct((t, d), BF16)],
        compiler_params=_cparams("arbitrary"),
        name="out_proj",
    )(attn, ssd, x, w_attn, w_ssd, n_post.reshape(1, d), gate, n_pre.reshape(1, d), scale, shift)


def _route_kernel(h_ref, wq_ref, sk_ref, a_ref, b_ref, g_ref):
    tt = h_ref.shape[0]
    kd = sk_ref.shape[2]
    q = _dot(h_ref[...], wq_ref[...])
    key_iota = lax.broadcasted_iota(jnp.int32, (PEER_KEYS, tt), 0)
    k8 = PEER_TOPK // 2
    row8 = lax.broadcasted_iota(jnp.int32, (k8, tt), 0)
    row16 = lax.broadcasted_iota(jnp.int32, (PEER_TOPK, tt), 0)
    flat = [row16 * PEER_TOPK] + [row8 * PEER_TOPK + kb for kb in range(1, k8)] + [row8 + k8]
    cand_flat = jnp.concatenate(flat, axis=0)
    a_rows, b_rows, g_rows = [], [], []
    for h in range(PEER_HEADS):
        tops = []
        for s in range(2):
            qhs = q[:, (2 * h + s) * kd:(2 * h + s + 1) * kd]
            sc_t = _dot3(sk_ref[s], qhs, _dot_nt)
            tops.append(_topk_rows(sc_t, PEER_TOPK, key_iota))
        (s1, i1), (s2, i2) = tops
        pieces = [s1 + s2[0:1, :]]
        for kb in range(1, k8):
            pieces.append(jnp.where(row8 < PEER_TOPK // (kb + 1), s1[0:k8, :] + s2[kb:kb + 1, :], -jnp.inf))
        pieces.append(s1[0:1, :] + s2[k8:, :])
        top, pos = _topk_rows(jnp.concatenate(pieces, axis=0), PEER_TOPK, cand_flat)
        pa, pb = pos >> 4, pos & (PEER_TOPK - 1)
        ai = jnp.zeros_like(pos)
        bi = jnp.zeros_like(pos)
        for kk in range(PEER_TOPK):
            ai = jnp.where(pa == kk, i1[kk:kk + 1, :], ai)
            bi = jnp.where(pb == kk, i2[kk:kk + 1, :], bi)
        e = jnp.exp(top - jnp.max(top, axis=0, keepdims=True))
        g_rows.append(e / jnp.sum(e, axis=0, keepdims=True))
        a_rows.append(ai)
        b_rows.append(bi)
    a_ref[...] = jnp.concatenate(a_rows, axis=0).astype(F32).T.astype(jnp.int32)
    b_ref[...] = jnp.concatenate(b_rows, axis=0).astype(F32).T.astype(jnp.int32)
    g_ref[...] = jnp.concatenate(g_rows, axis=0).T


def _peer_route(h2, wq, sub_keys, tt):
    t, d = h2.shape
    nq = wq.shape[1]
    ns = PEER_HEADS * PEER_TOPK
    assert ns == LANES and sub_keys.shape[1] == PEER_KEYS == LANES and PEER_TOPK == 16
    tok = pl.BlockSpec((tt, ns), lambda i: (i, 0))
    return pl.pallas_call(
        _route_kernel,
        grid=(t // tt,),
        in_specs=[pl.BlockSpec((tt, d), lambda i: (i, 0)),
                  pl.BlockSpec((d, nq), lambda i: (0, 0)),
                  pl.BlockSpec(sub_keys.shape, lambda i: (0, 0, 0))],
        out_specs=[tok, tok, tok],
        out_shape=[jax.ShapeDtypeStruct((t, ns), jnp.int32), jax.ShapeDtypeStruct((t, ns), jnp.int32),
                   jax.ShapeDtypeStruct((t, ns), F32)],
        compiler_params=_cparams("arbitrary"),
        name="peer_route",
    )(h2, wq, sub_keys)


def _peer_kernel(h_ref, a_ref, b_ref, g_ref, ulo_ref, uhi_ref, vlo_ref, vhi_ref, o_ref, w_scr, *, na):
    j = pl.program_id(1)
    tt = h_ref.shape[0]
    nk = PEER_KEYS
    half = nk // 2

    @pl.when(j == 0)
    def _():
        sub = lax.broadcasted_iota(jnp.int32, (nk, LANES), 0)

        def build(tb, carry):
            for r in range(PEER_BUILD_UNROLL):
                t = tb * PEER_BUILD_UNROLL + r
                arow = a_ref[pl.ds(t, 1), :]
                brow = b_ref[pl.ds(t, 1), :]
                grow = g_ref[pl.ds(t, 1), :]
                oa = jnp.where(sub == arow, 1.0, 0.0).astype(BF16)
                zb = jnp.where(sub == brow, grow, 0.0).astype(BF16)
                w_scr[pl.ds(pl.multiple_of(t * W_PITCH, 8), nk), :] = _dot_nt(oa, zb)
            return carry

        lax.fori_loop(0, tt // PEER_BUILD_UNROLL, build, 0)

    x = h_ref[...]

    def weights(a0):
        return jnp.concatenate([w_scr[pl.ds(a0 + a, tt, stride=W_PITCH), :] for a in range(na)], axis=1)

    p_lo = (weights(j * na) * jax.nn.gelu(_dot(x, ulo_ref[...]))).astype(BF16)
    p_hi = (weights(j * na + half) * jax.nn.gelu(_dot(x, uhi_ref[...]))).astype(BF16)
    contrib = _dot(p_lo, vlo_ref[...]) + _dot(p_hi, vhi_ref[...])

    @pl.when(j == 0)
    def _():
        o_ref[...] = contrib

    @pl.when(j != 0)
    def _():
        o_ref[...] += contrib


def _peer_experts(h2, a_idx, b_idx, g, u_t, v, tt, na):
    t, d = h2.shape
    ne = u_t.shape[1]
    eb = na * PEER_KEYS
    nsteps = ne // (2 * eb)
    assert ne == PEER_KEYS * PEER_KEYS and nsteps * 2 * eb == ne and t % tt == 0 and tt % PEER_BUILD_UNROLL == 0
    ns = a_idx.shape[1]
    tok = pl.BlockSpec((tt, ns), lambda i, j: (i, 0))
    lo = pl.BlockSpec((eb, d), lambda i, j: (j, 0))
    hi = pl.BlockSpec((eb, d), lambda i, j: (j + nsteps, 0))
    return pl.pallas_call(
        functools.partial(_peer_kernel, na=na),
        grid=(t // tt, nsteps),
        in_specs=[pl.BlockSpec((tt, d), lambda i, j: (i, 0)), tok, tok, tok,
                  pl.BlockSpec((d, eb), lambda i, j: (0, j)), pl.BlockSpec((d, eb), lambda i, j: (0, j + nsteps)),
                  lo, hi],
        out_specs=pl.BlockSpec((tt, d), lambda i, j: (i, 0)),
        out_shape=jax.ShapeDtypeStruct((t, d), F32),
        scratch_shapes=[pltpu.VMEM((tt * W_PITCH, LANES), F32)],
        compiler_params=_cparams("arbitrary", "arbitrary"),
        name="peer_experts",
    )(h2, a_idx, b_idx, g, u_t, u_t, v, v)


def _final_kernel(x_ref, f_ref, nw_ref, gate_ref, o_ref):
    o_ref[...] = x_ref[...] + gate_ref[...] * _rms(f_ref[...], nw_ref[...])


def _final(x1, ffn, row0, n_post, gate, tm):
    t, d = x1.shape
    assert row0 % tm == 0
    blk0 = row0 // tm
    mr = gate.shape[0]
    mod_spec = (pl.BlockSpec((1, d), lambda i: (0, 0)) if mr == 1 else pl.BlockSpec((tm, d), lambda i: (i, 0)))
    return pl.pallas_call(
        _final_kernel,
        grid=(t // tm,),
        in_specs=[pl.BlockSpec((tm, d), lambda i: (i, 0)), pl.BlockSpec((tm, d), lambda i: (i + blk0, 0)),
                  pl.BlockSpec((1, d), lambda i: (0, 0)), mod_spec],
        out_specs=pl.BlockSpec((tm, d), lambda i: (i, 0)),
        out_shape=jax.ShapeDtypeStruct((t, d), F32),
        compiler_params=_cparams("arbitrary"),
        name="final_residual",
    )(x1, ffn, n_post.reshape(1, d), gate)


def _tile(n, pref):
    t = min(n, pref)
    while n % t:
        t //= 2
    return t


def kernel(x_prompt, x_sample, c_prompt, c_sample, cache_k, cache_v, page_table, state_conv, state_ssm,
           w_ada, b_ada, norm_mix_pre, norm_mix_post, norm_ffn_pre, norm_ffn_post, w_in, conv_w, conv_b,
           dt_bias, a_log, d_skip, ssd_norm_w, w_out, peer_w_query, peer_sub_keys, peer_u, peer_v):
    bp, lp, d = x_prompt.shape
    bs, ls, _ = x_sample.shape
    assert bp == 1 and lp % MOBA_BLOCK == 0 and ls <= SSD_CHUNK
    attn_w = d // 2
    ssd_w = d - attn_w
    nh_a = attn_w // HEAD_DIM
    nh_s = ssd_w // SSD_HEAD_DIM
    cch = conv_w.shape[1]
    n_main = 3 * attn_w + ssd_w + cch
    n_pool, page, _, _ = cache_k.shape
    tp, ts = bp * lp, bs * ls

    rows = bp + bs
    rpad = -rows % 8
    c_all = jnp.pad(jnp.concatenate([c_prompt, c_sample], axis=0), ((0, rpad), (0, 0)))
    mod = _modulation(c_all, w_ada, b_ada)
    mod_p = [mod[0:bp, k * d:(k + 1) * d] for k in range(6)]
    mod_s = [jnp.repeat(mod[bp:rows, k * d:(k + 1) * d], ls, axis=0) for k in range(6)]

    w_in_t = w_in.T
    w_main = w_in_t[:n_main].astype(BF16)
    w_dt = jnp.pad(w_in_t[n_main:], ((0, LANES - nh_s), (0, 0))).astype(BF16)
    w_attn = w_out[:attn_w].astype(BF16)
    w_ssd = w_out[attn_w:].astype(BF16)
    wq = peer_w_query.astype(BF16)
    u_b = peer_u.T.astype(BF16)
    v_b = peer_v.astype(BF16)

    xp = x_prompt.reshape(tp, d)
    xs = x_sample.reshape(ts, d)
    tn = _tile(n_main, 512)
    proj_p, dt_p = _in_proj(xp, norm_mix_pre, mod_p[1], mod_p[0], w_main, w_dt, _tile(tp, 1024), tn)
    proj_s, dt_s = _in_proj(xs, norm_mix_pre, mod_s[1], mod_s[0], w_main, w_dt, _tile(ts, 256),
                            _tile(n_main, n_main // 2))

    attn_p = _moba_prompt(proj_p, lp, attn_w)
    ck_t = jnp.transpose(cache_k, (0, 2, 3, 1)).reshape(n_pool, attn_w, page)
    cv_t = jnp.transpose(cache_v, (0, 2, 3, 1)).reshape(n_pool, attn_w, page)
    attn_s = _moba_sample(proj_s.reshape(bs, ls, n_main), ck_t, cv_t, page_table, attn_w).reshape(ts, attn_w)

    conv0_p = jnp.zeros((bp, SSD_CONV - 1, cch), F32)
    ssm0_p = jnp.zeros((bp, ssd_w, SSD_STATE), F32)
    ssd_args = (conv_w, conv_b, dt_bias, a_log, d_skip, ssd_norm_w)
    ssd_p, ssm_p = _ssd(proj_p, dt_p, conv0_p, ssm0_p, *ssd_args, bp, lp, ssd_w)
    ssd_s, ssm_s = _ssd(proj_s, dt_s, state_conv, state_ssm.reshape(bs, ssd_w, SSD_STATE), *ssd_args,
                        bs, ls, ssd_w)

    x1_p, h2_p = _out_proj(attn_p, ssd_p, xp, w_attn, w_ssd, norm_mix_post, mod_p[2], norm_ffn_pre,
                           mod_p[4], mod_p[3], _tile(tp, 512))
    x1_s, h2_s = _out_proj(attn_s, ssd_s, xs, w_attn, w_ssd, norm_mix_post, mod_s[2], norm_ffn_pre,
                           mod_s[4], mod_s[3], _tile(ts, 256))

    h2 = jnp.concatenate([h2_p, h2_s], axis=0)
    a_idx, b_idx, g = _peer_route(h2, wq, peer_sub_keys, _tile(tp + ts, 256))
    tt_e = PEER_TILES[0] if (tp + ts) % PEER_TILES[0] == 0 else _tile(tp + ts, 256)
    ffn = _peer_experts(h2, a_idx, b_idx, g, u_b, v_b, tt_e, PEER_TILES[1])

    y_p = _final(x1_p, ffn, 0, norm_ffn_post, mod_p[5], _tile(tp, 512))
    y_s = _final(x1_s, ffn, tp, norm_ffn_post, mod_s[5], _tile(ts, 256))

    kcol, vcol, xcol = attn_w, 2 * attn_w, 3 * attn_w + ssd_w
    k_p = proj_p[:, kcol:kcol + attn_w].reshape(bp, lp, nh_a, HEAD_DIM)
    v_p = proj_p[:, vcol:vcol + attn_w].reshape(bp, lp, nh_a, HEAD_DIM)
    conv_p = proj_p.reshape(bp, lp, n_main)[:, lp - (SSD_CONV - 1):, xcol:xcol + cch]
    k_s = proj_s[:, kcol:kcol + attn_w].reshape(bs, ls, nh_a, HEAD_DIM)
    v_s = proj_s[:, vcol:vcol + attn_w].reshape(bs, ls, nh_a, HEAD_DIM)
    tail = min(ls, SSD_CONV - 1)
    xbc_s = proj_s.reshape(bs, ls, n_main)[:, ls - tail:, xcol:xcol + cch]
    conv_s = jnp.concatenate([state_conv[:, ls:], xbc_s], axis=1)
    return (y_p.reshape(bp, lp, d), y_s.reshape(bs, ls, d), k_p, v_p, conv_p,
            ssm_p.reshape(bp, nh_s, SSD_HEAD_DIM, SSD_STATE), k_s, v_s, conv_s,
            ssm_s.reshape(bs, nh_s, SSD_HEAD_DIM, SSD_STATE))
```
